```python
import math
import jax, jax.numpy as jnp
from jax import lax
import numpy as np

D_MODEL = 2048
BATCH = 8
SEQ = 4096
DEPTH = 2

CHUNK = 64
Q_BLOCK = 128
MIX_WIDTH = D_MODEL // 2
N_BRANCH = 3
H_A = MIX_WIDTH // 128
DH_A = 64
H_B = MIX_WIDTH // 128
DK_B = MIX_WIDTH // H_B
DV_B = MIX_WIDTH // H_B
H_C = 4
DK_C = MIX_WIDTH // (2 * H_C)
DV_C = MIX_WIDTH // H_C
GLA_GATE_RANK = 16
GLA_GATE_NORM = 16.0
D_FF = 11 * D_MODEL // 4
N_BUCKETS = 32
MAX_DISTANCE = 128
EPS = 1e-6

IN_SIZES = (
    H_A * 2 * DH_A, H_A * 2 * DH_A, H_A * 2 * DH_A,
    H_B * DK_B, H_B * DK_B, H_B * DV_B, H_B * DV_B,
    H_C * DK_C, H_C * DK_C, H_C * DV_C, H_C * DV_C, GLA_GATE_RANK,
    N_BRANCH * D_MODEL,
)
D_IN = sum(IN_SIZES)

kernel_name = 'chunk_causal_hybrid_diffattn_hgrn2_gla_macaron'


def rms_norm(x, gain):
    x32 = x.astype(jnp.float32)
    y = x32 * lax.rsqrt(jnp.mean(x32 * x32, axis=-1, keepdims=True) + EPS)
    return (y * gain.astype(jnp.float32)).astype(x.dtype)


def modulate(x, gain, shift, scale):
    return rms_norm(x, gain) * (1.0 + scale[:, None, :]) + shift[:, None, :]


def swiglu(h, w_gate, w_up, w_down):
    return (jax.nn.silu(h @ w_gate) * (h @ w_up)) @ w_down


def split_cols(u):
    out, idx = [], 0
    for size in IN_SIZES:
        out.append(u[..., idx:idx + size])
        idx += size
    return out


def t5_bucket(rel):
    half = N_BUCKETS // 2
    max_exact = half // 2
    ret = jnp.where(rel > 0, half, 0)
    n = jnp.abs(rel)
    nf = jnp.maximum(n, 1).astype(jnp.float32)
    large = max_exact + (jnp.log(nf / max_exact) / math.log(MAX_DISTANCE / max_exact)
                         * (half - max_exact)).astype(jnp.int32)
    large = jnp.minimum(large, half - 1)
    return ret + jnp.where(n < max_exact, n, large)


def diff_attention(q, k, v, rel_bias, lam, lam_init, out_gain):
    seq = q.shape[1]
    scale = DH_A ** -0.5
    outs = []
    for blk in range(seq // Q_BLOCK):
        q0, end = blk * Q_BLOCK, (blk + 1) * Q_BLOCK
        logits = jnp.einsum('bqhmd,bkhmd->bhmqk', q[:, q0:end], k[:, :end]).astype(jnp.float32) * scale
        qpos = jnp.arange(q0, end)
        kpos = jnp.arange(end)
        bias = rel_bias[t5_bucket(kpos[None, :] - qpos[:, None])].astype(jnp.float32)
        logits = logits + jnp.transpose(bias, (2, 0, 1))[None, :, None]
        visible = (kpos[None, :] // CHUNK) <= (qpos[:, None] // CHUNK)
        probs = jax.nn.softmax(jnp.where(visible, logits, -jnp.inf), axis=-1)
        weights = probs[:, :, 0] - lam * probs[:, :, 1]
        outs.append(jnp.einsum('bhqk,bkhe->bqhe', weights.astype(v.dtype), v[:, :end]))
    o = jnp.concatenate(outs, axis=1)
    return rms_norm(o, out_gain) * (1.0 - lam_init)


def chunk_gated_linear_attention(q, k, v, log_f):
    bsz, seq, heads, dk = q.shape
    dv = v.shape[-1]
    n = seq // CHUNK

    def to_chunks(t):
        return t.astype(jnp.float32).reshape(bsz, n, CHUNK, heads, t.shape[-1]).transpose(1, 0, 3, 2, 4)

    qc, kc, vc, gc = to_chunks(q), to_chunks(k), to_chunks(v), to_chunks(log_f)
    causal = jnp.tril(jnp.ones((CHUNK, CHUNK), dtype=bool))

    def step(state, inp):
        qi, ki, vi, gi = inp
        b = jnp.cumsum(gi, axis=-2)
        diff = b[..., :, None, :] - b[..., None, :, :]
        decay = jnp.exp(jnp.where(causal[:, :, None], diff, -jnp.inf))
        scores = jnp.einsum('bhtd,bhsd,bhtsd->bhts', qi, ki, decay)
        o = jnp.einsum('bhts,bhsv->bhtv', scores, vi) + jnp.einsum('bhtd,bhdv->bhtv', qi * jnp.exp(b), state)
        b_last = b[..., -1:, :]
        state = jnp.exp(b_last[..., 0, :])[..., :, None] * state + jnp.einsum(
            'bhsd,bhsv->bhdv', ki * jnp.exp(b_last - b), vi)
        return state, o

    state0 = jnp.zeros((bsz, heads, dk, dv), jnp.float32)
    _, o = lax.scan(step, state0, (qc, kc, vc, gc))
    return o.transpose(1, 0, 3, 2, 4).reshape(bsz, seq, heads, dv).astype(v.dtype)


def hybrid_mixer(h, lam_init, w_in, qk_gain, lam_vec, diff_gain, rel_bias, lb,
                 hgrn_gain, gla_w_up, gla_b, gla_gain, w_branch, w_out):
    bsz, seq = h.shape[0], h.shape[1]
    u = h @ w_in
    aq, ak, av, bq, bf, bi, bg, cq, ck, cv, cr, cgd, gate_logits = split_cols(u)

    aq = rms_norm(aq.reshape(bsz, seq, H_A, 2, DH_A), qk_gain[0])
    ak = rms_norm(ak.reshape(bsz, seq, H_A, 2, DH_A), qk_gain[1])
    lv = lam_vec.astype(jnp.float32)
    lam = jnp.exp(jnp.sum(lv[0] * lv[1])) - jnp.exp(jnp.sum(lv[2] * lv[3])) + lam_init
    ya = diff_attention(aq, ak, av.reshape(bsz, seq, H_A, 2 * DH_A), rel_bias, lam, lam_init,
                        diff_gain).reshape(bsz, seq, MIX_WIDTH)

    lb_h = lb.reshape(H_B, DK_B)
    zf = bf.reshape(bsz, seq, H_B, DK_B).astype(jnp.float32)
    log_f = jnp.logaddexp(jnp.log(lb_h), jnp.log1p(-lb_h) + jax.nn.log_sigmoid(zf))
    k_b = (1.0 - lb_h) * jax.nn.sigmoid(-zf)
    ob = chunk_gated_linear_attention(bq.reshape(bsz, seq, H_B, DK_B), k_b,
                                      bi.reshape(bsz, seq, H_B, DV_B), log_f)
    yb = rms_norm(ob * jax.nn.sigmoid(bg).reshape(bsz, seq, H_B, DV_B), hgrn_gain).reshape(bsz, seq, MIX_WIDTH)

    log_a = jax.nn.log_sigmoid((cgd @ gla_w_up + gla_b).astype(jnp.float32)) / GLA_GATE_NORM
    oc = chunk_gated_linear_attention(cq.reshape(bsz, seq, H_C, DK_C) * (DK_C ** -0.5),
                                      ck.reshape(bsz, seq, H_C, DK_C),
                                      cv.reshape(bsz, seq, H_C, DV_C),
                                      log_a.reshape(bsz, seq, H_C, DK_C))
    yc = (rms_norm(oc, gla_gain) * jax.nn.silu(cr.reshape(bsz, seq, H_C, DV_C))).reshape(bsz, seq, MIX_WIDTH)

    ys = jnp.stack([ya, yb, yc], axis=2)
    z = jnp.einsum('bsnc,ncd->bsnd', ys, w_branch)
    g = jax.nn.sigmoid(gate_logits.reshape(bsz, seq, N_BRANCH, D_MODEL))
    return jnp.sum(g * z, axis=2) @ w_out


def setup_inputs(seed: int = 0) -> dict:
    key = jax.random.key(seed)
    ks = jax.random.split(key, 20)

    def nrm(k, shape, scale):
        return jax.random.normal(k, shape, jnp.float32) * scale

    return {
        'x': nrm(ks[0], (BATCH, SEQ, D_MODEL), 1.0),
        'c': nrm(ks[1], (BATCH, D_MODEL), 1.0),
        'w_ada': nrm(ks[2], (DEPTH, D_MODEL, 9 * D_MODEL), 0.5 * D_MODEL ** -0.5),
        'b_ada': nrm(ks[3], (DEPTH, 9 * D_MODEL), 0.02),
        'norm_gains': 1.0 + nrm(ks[4], (DEPTH, 4, D_MODEL), 0.02),
        'ffn_w_gate': nrm(ks[5], (DEPTH, 2, D_MODEL, D_FF), D_MODEL ** -0.5),
        'ffn_w_up': nrm(ks[6], (DEPTH, 2, D_MODEL, D_FF), D_MODEL ** -0.5),
        'ffn_w_down': nrm(ks[7], (DEPTH, 2, D_FF, D_MODEL), D_FF ** -0.5),
        'w_in': nrm(ks[8], (DEPTH, D_MODEL, D_IN), D_MODEL ** -0.5),
        'qk_gains': 1.0 + nrm(ks[9], (DEPTH, 2, DH_A), 0.02),
        'diff_lambda': nrm(ks[10], (DEPTH, 4, DH_A), 0.1),
        'diff_out_gain': 1.0 + nrm(ks[11], (DEPTH, 2 * DH_A), 0.02),
        'rel_bias': nrm(ks[12], (N_BUCKETS, H_A), 0.5),
        'hgrn_lb_logits': nrm(ks[13], (DEPTH, H_B * DK_B), 1.0),
        'hgrn_out_gain': 1.0 + nrm(ks[14], (DEPTH, DV_B), 0.02),
        'gla_w_gate_up': nrm(ks[15], (DEPTH, GLA_GATE_RANK, H_C * DK_C), GLA_GATE_RANK ** -0.5),
        'gla_b_gate': nrm(ks[16], (DEPTH, H_C * DK_C), 0.1),
        'gla_out_gain': 1.0 + nrm(ks[17], (DEPTH, DV_C), 0.02),
        'w_branch': nrm(ks[18], (DEPTH, N_BRANCH, MIX_WIDTH, D_MODEL), MIX_WIDTH ** -0.5),
        'w_out': nrm(ks[19], (DEPTH, D_MODEL, D_MODEL), D_MODEL ** -0.5),
    }


def reference(x, c, w_ada, b_ada, norm_gains, ffn_w_gate, ffn_w_up, ffn_w_down, w_in,
              qk_gains, diff_lambda, diff_out_gain, rel_bias, hgrn_lb_logits, hgrn_out_gain,
              gla_w_gate_up, gla_b_gate, gla_out_gain, w_branch, w_out):
    lb_all = jnp.cumsum(jax.nn.softmax(hgrn_lb_logits.astype(jnp.float32), axis=0), axis=0)
    lb_all = lb_all - lb_all[0]
    cond = jax.nn.silu(c)
    bsz = c.shape[0]
    for l in range(DEPTH):
        mod = (cond @ w_ada[l] + b_ada[l]).reshape(bsz, 3, 3, D_MODEL)
        shift, scale, gate = mod[:, :, 0], mod[:, :, 1], mod[:, :, 2]
        lam_init = 0.8 - 0.6 * math.exp(-0.3 * l)

        h = modulate(x, norm_gains[l, 0], shift[:, 0], scale[:, 0])
        x = x + 0.5 * gate[:, 0, None, :] * swiglu(h, ffn_w_gate[l, 0], ffn_w_up[l, 0], ffn_w_down[l, 0])

        h = modulate(x, norm_gains[l, 1], shift[:, 1], scale[:, 1])
        x = x + gate[:, 1, None, :] * hybrid_mixer(
            h, lam_init, w_in[l], qk_gains[l], diff_lambda[l], diff_out_gain[l], rel_bias,
            lb_all[l], hgrn_out_gain[l], gla_w_gate_up[l], gla_b_gate[l], gla_out_gain[l],
            w_branch[l], w_out[l])

        h = modulate(x, norm_gains[l, 2], shift[:, 2], scale[:, 2])
        x = x + 0.5 * gate[:, 2, None, :] * swiglu(h, ffn_w_gate[l, 1], ffn_w_up[l, 1], ffn_w_down[l, 1])

        x = rms_norm(x, norm_gains[l, 3])
    return x
```

```python
import functools
import math

import jax
import jax.numpy as jnp
from jax import lax
from jax.experimental import pallas as pl
from jax.experimental.pallas import tpu as pltpu

D_MODEL = 2048
DEPTH = 2
CHUNK = 64
MIX_WIDTH = D_MODEL // 2
N_BRANCH = 3
H_A = 8
DH_A = 64
H_B = 8
DK_B = 128
DV_B = 128
H_C = 4
DK_C = 128
DV_C = 256
GLA_GATE_RANK = 16
GLA_GATE_NORM = 16.0
D_FF = 11 * D_MODEL // 4
N_BUCKETS = 32
MAX_DISTANCE = 128
EPS = 1e-6

LANE = 128
MASK_VALUE = -1e30
VMEM_LIMIT = 56 * 1024 * 1024

N_GATE_BLK = N_BRANCH * D_MODEL // LANE
BLK_AQ = N_GATE_BLK
BLK_AK = BLK_AQ + H_A
BLK_AV = BLK_AK + H_A
BLK_BQ = BLK_AV + H_A
BLK_BF = BLK_BQ + H_B
BLK_BI = BLK_BF + H_B
BLK_BG = BLK_BI + H_B
BLK_CQ = BLK_BG + H_B
BLK_CK = BLK_CQ + H_C
BLK_CV = BLK_CK + H_C
BLK_CR = BLK_CV + 2 * H_C
BLK_CGD = BLK_CR + 2 * H_C
N_MAIN = 10 * MIX_WIDTH
IN_TILE = 4 * LANE
N_BLK = BLK_CGD + IN_TILE // LANE
N_IN_PAD = N_BLK * LANE

BF16 = jnp.bfloat16
F32 = jnp.float32


def _params(*sem):
    return pltpu.CompilerParams(dimension_semantics=sem, vmem_limit_bytes=VMEM_LIMIT)


def _sigmoid(x):
    return 1.0 / (1.0 + jnp.exp(-x))


def _log_sigmoid(x):
    return jnp.minimum(x, 0.0) - jnp.log1p(jnp.exp(-jnp.abs(x)))


def _rms(x):
    return x * lax.rsqrt(jnp.mean(x * x, axis=-1, keepdims=True) + EPS)


def _dot(a, b):
    return jnp.dot(a, b, preferred_element_type=F32)


def _dot_nt(a, b):
    return lax.dot_general(a, b, (((1,), (1,)), ((), ())), preferred_element_type=F32)


def _dot_tn(a, b):
    return lax.dot_general(a, b, (((0,), (0,)), ((), ())), preferred_element_type=F32)


def _ada_kernel(c_ref, w_ref, b_ref, o_ref):
    c = c_ref[...]
    cond = c * _sigmoid(c)
    c_hi = cond.astype(BF16)
    c_lo = (cond - c_hi.astype(F32)).astype(BF16)
    w = w_ref[0]
    w_hi = w.astype(BF16)
    w_lo = (w - w_hi.astype(F32)).astype(BF16)
    o_ref[0] = _dot(c_hi, w_hi) + (_dot(c_lo, w_hi) + _dot(c_hi, w_lo)) + b_ref[0]


def _ada(c, w_ada, b_ada):
    bsz = c.shape[0]
    n = w_ada.shape[-1]
    tn = 1024
    return pl.pallas_call(
        _ada_kernel,
        grid=(DEPTH, n // tn),
        in_specs=[
            pl.BlockSpec((bsz, D_MODEL), lambda l, j: (0, 0)),
            pl.BlockSpec((1, D_MODEL, tn), lambda l, j: (l, 0, j)),
            pl.BlockSpec((1, 1, tn), lambda l, j: (l, 0, j)),
        ],
        out_specs=pl.BlockSpec((1, bsz, tn), lambda l, j: (l, 0, j)),
        out_shape=jax.ShapeDtypeStruct((DEPTH, bsz, n), F32),
        compiler_params=_params("parallel", "parallel"),
        name="ada_mod",
    )(c, w_ada, b_ada.reshape(DEPTH, 1, n))


def _modulated(x, gain, scale, shift):
    return (_rms(x) * gain) * (1.0 + scale) + shift


def _ffn_kernel(x_ref, shift_ref, scale_ref, gate_ref, gain_ref, fgain_ref, wg_ref, wu_ref, wd_ref,
                o_ref, h_ref, acc_ref, *, final_norm):
    j = pl.program_id(1)

    @pl.when(j == 0)
    def _():
        h = _modulated(x_ref[...], gain_ref[...], scale_ref[0], shift_ref[0])
        h_ref[...] = h.astype(BF16)
        acc_ref[...] = jnp.zeros_like(acc_ref)

    h = h_ref[...]
    g = _dot(h, wg_ref[...])
    u = _dot(h, wu_ref[...])
    a = (g * _sigmoid(g) * u).astype(BF16)
    acc_ref[...] += _dot(a, wd_ref[...])

    @pl.when(j == pl.num_programs(1) - 1)
    def _():
        y = x_ref[...] + 0.5 * gate_ref[0] * acc_ref[...]
        if final_norm:
            y = _rms(y) * fgain_ref[...]
        o_ref[...] = y


def _ffn(x2, seq, shift, scale, gate, gain, fgain, wg, wu, wd, *, final_norm):
    t = x2.shape[0]
    tm = min(512, seq)
    tf = 512
    per_b = seq // tm
    row = pl.BlockSpec((tm, D_MODEL), lambda i, j: (i, 0))
    vec_b = pl.BlockSpec((1, 1, D_MODEL), lambda i, j: (i // per_b, 0, 0))
    vec = pl.BlockSpec((1, D_MODEL), lambda i, j: (0, 0))
    return pl.pallas_call(
        functools.partial(_ffn_kernel, final_norm=final_norm),
        grid=(t // tm, D_FF // tf),
        in_specs=[row, vec_b, vec_b, vec_b, vec, vec,
                  pl.BlockSpec((D_MODEL, tf), lambda i, j: (0, j)),
                  pl.BlockSpec((D_MODEL, tf), lambda i, j: (0, j)),
                  pl.BlockSpec((tf, D_MODEL), lambda i, j: (j, 0))],
        out_specs=row,
        out_shape=jax.ShapeDtypeStruct((t, D_MODEL), F32),
        scratch_shapes=[pltpu.VMEM((tm, D_MODEL), BF16), pltpu.VMEM((tm, D_MODEL), F32)],
        compiler_params=_params("parallel", "arbitrary"),
        name="ffn",
    )(x2, shift, scale, gate, gain, fgain, wg, wu, wd)


def _inproj_kernel(x_ref, shift_ref, scale_ref, gain_ref, w_ref, qkg_ref, o_ref, h_ref):
    j = pl.program_id(1)

    @pl.when(j == 0)
    def _():
        h = _modulated(x_ref[...], gain_ref[...], scale_ref[0], shift_ref[0])
        h_ref[...] = h.astype(BF16)

    res = _dot(h_ref[...], w_ref[...])
    first_qk = BLK_AQ * LANE // IN_TILE
    last_qk = BLK_AV * LANE // IN_TILE
    is_qk = jnp.logical_and(j >= first_qk, j < last_qk)

    @pl.when(is_qk)
    def _():
        lane = lax.broadcasted_iota(jnp.int32, (1, LANE), 1)
        low = lane < DH_A
        for s in range(IN_TILE // LANE):
            r = res[:, s * LANE:(s + 1) * LANE]
            sq = r * r
            lo = jnp.sum(jnp.where(low, sq, 0.0), axis=-1, keepdims=True)
            hi = jnp.sum(jnp.where(low, 0.0, sq), axis=-1, keepdims=True)
            ms = jnp.where(low, lo, hi) * (1.0 / DH_A)
            o_ref[s] = (r * lax.rsqrt(ms + EPS) * qkg_ref[0]).astype(o_ref.dtype)

    @pl.when(jnp.logical_not(is_qk))
    def _():
        for s in range(IN_TILE // LANE):
            o_ref[s] = res[:, s * LANE:(s + 1) * LANE].astype(o_ref.dtype)


def _inproj(x2, seq, shift, scale, gain, w_in_p, qk_gain_tiles):
    t = x2.shape[0]
    tm = min(1024, seq)
    per_b = seq // tm
    nblk = IN_TILE // LANE
    first_qk = BLK_AQ * LANE // IN_TILE
    n_qk = 2 * H_A * LANE // IN_TILE
    return pl.pallas_call(
        _inproj_kernel,
        grid=(t // tm, N_IN_PAD // IN_TILE),
        in_specs=[
            pl.BlockSpec((tm, D_MODEL), lambda i, j: (i, 0)),
            pl.BlockSpec((1, 1, D_MODEL), lambda i, j: (i // per_b, 0, 0)),
            pl.BlockSpec((1, 1, D_MODEL), lambda i, j: (i // per_b, 0, 0)),
            pl.BlockSpec((1, D_MODEL), lambda i, j: (0, 0)),
            pl.BlockSpec((D_MODEL, IN_TILE), lambda i, j: (0, j)),
            pl.BlockSpec((1, 1, LANE), lambda i, j: (jnp.clip(j - first_qk, 0, n_qk - 1), 0, 0)),
        ],
        out_specs=pl.BlockSpec((nblk, tm, LANE), lambda i, j: (j, i, 0)),
        out_shape=jax.ShapeDtypeStruct((N_BLK, t, LANE), BF16),
        scratch_shapes=[pltpu.VMEM((tm, D_MODEL), BF16)],
        compiler_params=_params("parallel", "arbitrary"),
        name="in_proj",
    )(x2, shift, scale, gain, w_in_p, qk_gain_tiles)


def _t5_bucket(rel):
    half = N_BUCKETS // 2
    max_exact = half // 2
    ret = jnp.where(rel > 0, half, 0)
    n = jnp.abs(rel)
    nf = jnp.maximum(n, 1).astype(F32)
    large = max_exact + (jnp.log(nf / max_exact) / math.log(MAX_DISTANCE / max_exact)
                         * (half - max_exact)).astype(jnp.int32)
    large = jnp.minimum(large, half - 1)
    return ret + jnp.where(n < max_exact, n, large)


def _far_bucket_is_constant(min_dist):
    half = N_BUCKETS // 2
    max_exact = half // 2
    val = math.log(min_dist / max_exact) / math.log(MAX_DISTANCE / max_exact) * (half - max_exact)
    return max_exact + val >= half


def _bias_tiles(rel_bias, tq):
    assert _far_bucket_is_constant(tq + 1)
    qpos = jnp.arange(tq)[:, None]
    kpos = jnp.arange(tq)[None, :]
    far = rel_bias[N_BUCKETS // 2 - 1].astype(F32)
    b0 = rel_bias[_t5_bucket(kpos - qpos)].astype(F32) - far
    b1 = rel_bias[_t5_bucket(kpos - tq - qpos)].astype(F32) - far
    visible = (kpos // CHUNK) <= (qpos // CHUNK)
    b0 = jnp.where(visible[:, :, None], b0, MASK_VALUE)
    b0 = jnp.transpose(b0, (2, 0, 1))
    b1 = jnp.transpose(b1, (2, 0, 1))
    return jnp.concatenate([b0, b0], axis=1), jnp.concatenate([b1, b1], axis=1)


def _attn_kernel(lv_ref, og_ref, q_ref, k_ref, v_ref, b0_ref, b1_ref, o_ref, m_ref, l_ref, acc_ref,
                 *, tq, lam_init):
    qi = pl.program_id(2)
    q = q_ref[0, 0]
    lane = lax.broadcasted_iota(jnp.int32, (1, LANE), 1)
    keep0 = jnp.where(lane < DH_A, 1.0, 0.0).astype(q.dtype)
    keep1 = jnp.where(lane < DH_A, 0.0, 1.0).astype(q.dtype)
    qs = jnp.concatenate([q * keep0, q * keep1], axis=0)

    m_ref[...] = jnp.full_like(m_ref, MASK_VALUE)
    l_ref[...] = jnp.zeros_like(l_ref)
    acc_ref[...] = jnp.zeros_like(acc_ref)

    def update(kstart, bias):
        kb = k_ref[0, 0, pl.ds(kstart, tq), :]
        vb = v_ref[0, 0, pl.ds(kstart, tq), :]
        s = _dot_nt(qs, kb)
        if bias is not None:
            s = s + bias
        m_prev = m_ref[...]
        m_new = jnp.maximum(m_prev, jnp.max(s, axis=-1, keepdims=True))
        alpha = jnp.exp(m_prev - m_new)
        p = jnp.exp(s - m_new[:, :1])
        l_ref[...] = alpha * l_ref[...] + jnp.sum(p, axis=-1, keepdims=True)
        m_ref[...] = m_new
        acc_ref[...] = alpha * acc_ref[...] + _dot(p.astype(vb.dtype), vb)

    update(pl.multiple_of(qi * tq, tq), b0_ref[0])

    @pl.when(qi >= 1)
    def _():
        update(pl.multiple_of((qi - 1) * tq, tq), b1_ref[0])

    def far(kj, carry):
        update(pl.multiple_of(kj * tq, tq), None)
        return carry

    lax.fori_loop(0, jnp.maximum(qi - 1, 0), far, 0)

    o = acc_ref[...] / l_ref[...]
    lv = lv_ref[...]
    lam = (jnp.exp(jnp.sum(lv[0:1] * lv[1:2], axis=-1, keepdims=True))
           - jnp.exp(jnp.sum(lv[2:3] * lv[3:4], axis=-1, keepdims=True)) + lam_init)
    od = o[:tq] - lam * o[tq:]
    o_ref[...] = (_rms(od) * og_ref[...] * (1.0 - lam_init)).astype(o_ref.dtype)


def _attention(u4, lam_vec, out_gain, bias0, bias1, *, lam_init):
    _, bsz, seq, _ = u4.shape
    tq = bias0.shape[-1]
    nq = seq // tq
    kv_spec = lambda base: pl.BlockSpec((1, 1, seq, LANE), lambda b, h, i: (base + h, b, 0, 0))
    bias_spec = pl.BlockSpec((1, 2 * tq, tq), lambda b, h, i: (h, 0, 0))
    return pl.pallas_call(
        functools.partial(_attn_kernel, tq=tq, lam_init=lam_init),
        grid=(bsz, H_A, nq),
        in_specs=[
            pl.BlockSpec((4, DH_A), lambda b, h, i: (0, 0)),
            pl.BlockSpec((1, LANE), lambda b, h, i: (0, 0)),
            pl.BlockSpec((1, 1, tq, LANE), lambda b, h, i: (BLK_AQ + h, b, i, 0)),
            kv_spec(BLK_AK), kv_spec(BLK_AV), bias_spec, bias_spec,
        ],
        out_specs=pl.BlockSpec((tq, LANE), lambda b, h, i: (b * nq + i, h)),
        out_shape=jax.ShapeDtypeStruct((bsz * seq, MIX_WIDTH), BF16),
        scratch_shapes=[pltpu.VMEM((2 * tq, LANE), F32)] * 3,
        compiler_params=_params("parallel", "parallel", "arbitrary"),
        name="diff_attn",
    )(lam_vec, out_gain.reshape(1, LANE), u4, u4, u4, bias0, bias1)


def _chunk_rows(x, rows, pick):
    r, d = x.shape
    g = x.reshape(r // rows, rows, d)[:, pick:pick + 1, :]
    return jnp.broadcast_to(g, (r // rows, rows, d)).reshape(r, d)


def _gla_block(q, k, v, g, st_ref):
    r_rows, dk = q.shape
    row = lax.broadcasted_iota(jnp.int32, (r_rows, 1), 0)
    loc = row % CHUNK
    col = lax.broadcasted_iota(jnp.int32, (1, r_rows), 1)

    b = g
    sh = 1
    while sh < CHUNK:
        b = b + jnp.where(loc >= sh, pltpu.roll(b, sh, axis=0), 0.0)
        sh *= 2

    scores = jnp.where(row == col, _dot_nt(q.astype(BF16), k.astype(BF16)), 0.0)
    w = CHUNK // 2
    while w >= 1:
        odd = (loc // w) % 2 == 1
        if w >= 4:
            ref_b = _chunk_rows(b, 2 * w, w - 1)
            d = jnp.where(odd, b - ref_b, ref_b - b)
        elif w == 2:
            i4 = loc % 4
            g_next = pltpu.roll(g, r_rows - 1, axis=0)
            g_prev = pltpu.roll(g, 1, axis=0)
            d = jnp.where(i4 == 0, g_next, jnp.where(i4 == 2, g, jnp.where(i4 == 3, g + g_prev, 0.0)))
        else:
            d = jnp.where(odd, g, 0.0)
        f = jnp.exp(d)
        qf = jnp.where(odd, q * f, 0.0).astype(BF16)
        kf = jnp.where(odd, 0.0, k * f).astype(BF16)
        same_pair = (row // (2 * w)) == (col // (2 * w))
        scores = scores + jnp.where(same_pair, _dot_nt(qf, kf), 0.0)
        w //= 2

    o_intra = _dot(scores.astype(BF16), v)

    b_last = _chunk_rows(b, CHUNK, CHUNK - 1)
    qe = (q * jnp.exp(b)).astype(BF16)
    ke = (k * jnp.exp(b_last - b)).astype(BF16)
    dec = jnp.exp(b_last)
    outs = []
    for c in range(r_rows // CHUNK):
        sl = slice(c * CHUNK, (c + 1) * CHUNK)
        st = st_ref[...]
        outs.append(_dot_nt(qe[sl], st.astype(BF16)))
        st_ref[...] = st * dec[c * CHUNK:c * CHUNK + 1] + _dot_tn(v[sl], ke[sl])
    return o_intra + jnp.concatenate(outs, axis=0)


def _hgrn_kernel(lb_ref, og_ref, q_ref, f_ref, i_ref, g_ref, o_ref, st_ref):
    @pl.when(pl.program_id(2) == 0)
    def _():
        st_ref[...] = jnp.zeros_like(st_ref)

    lb = lb_ref[0]
    zf = f_ref[0, 0].astype(F32)
    a = jnp.log(lb)
    c = jnp.log1p(-lb) + _log_sigmoid(zf)
    log_f = jnp.maximum(a, c) + jnp.log1p(jnp.exp(-jnp.abs(a - c)))
    k = (1.0 - lb) * _sigmoid(-zf)
    o = _gla_block(q_ref[0, 0].astype(F32), k, i_ref[0, 0], log_f, st_ref)
    o = o * _sigmoid(g_ref[0, 0].astype(F32))
    o_ref[...] = (_rms(o) * og_ref[...]).astype(o_ref.dtype)


def _hgrn(u4, lb, out_gain, tc):
    _, bsz, seq, _ = u4.shape
    nt = seq // tc
    blk = lambda base: pl.BlockSpec((1, 1, tc, LANE), lambda b, h, i: (base + h, b, i, 0))
    return pl.pallas_call(
        _hgrn_kernel,
        grid=(bsz, H_B, nt),
        in_specs=[
            pl.BlockSpec((1, 1, DK_B), lambda b, h, i: (h, 0, 0)),
            pl.BlockSpec((1, DV_B), lambda b, h, i: (0, 0)),
            blk(BLK_BQ), blk(BLK_BF), blk(BLK_BI), blk(BLK_BG),
        ],
        out_specs=pl.BlockSpec((tc, DV_B), lambda b, h, i: (b * nt + i, h)),
        out_shape=jax.ShapeDtypeStruct((bsz * seq, MIX_WIDTH), BF16),
        scratch_shapes=[pltpu.VMEM((DV_B, DK_B), F32)],
        compiler_params=_params("parallel", "parallel", "arbitrary"),
        name="hgrn2",
    )(lb.reshape(H_B, 1, DK_B), out_gain.reshape(1, DV_B), u4, u4, u4, u4)


def _glac_kernel(wup_ref, bup_ref, og_ref, q_ref, k_ref, v_ref, r_ref, gd_ref, o_ref, st_ref):
    @pl.when(pl.program_id(2) == 0)
    def _():
        st_ref[...] = jnp.zeros_like(st_ref)

    z = _dot(gd_ref[0, 0], wup_ref[...]) + bup_ref[0]
    log_a = _log_sigmoid(z) * (1.0 / GLA_GATE_NORM)
    q = q_ref[0, 0].astype(F32) * (DK_C ** -0.5)
    v = jnp.concatenate([v_ref[0, 0], v_ref[1, 0]], axis=1)
    o = _gla_block(q, k_ref[0, 0].astype(F32), v, log_a, st_ref)
    gate = jnp.concatenate([r_ref[0, 0], r_ref[1, 0]], axis=1).astype(F32)
    o_ref[...] = (_rms(o) * og_ref[...] * (gate * _sigmoid(gate))).astype(o_ref.dtype)


def _glac(u4, w_up_p, b_up, out_gain, tc):
    _, bsz, seq, _ = u4.shape
    nt = seq // tc
    one = lambda base: pl.BlockSpec((1, 1, tc, LANE), lambda b, h, i: (base + h, b, i, 0))
    two = lambda base: pl.BlockSpec((2, 1, tc, LANE), lambda b, h, i: (base // 2 + h, b, i, 0))
    return pl.pallas_call(
        _glac_kernel,
        grid=(bsz, H_C, nt),
        in_specs=[
            pl.BlockSpec((LANE, DK_C), lambda b, h, i: (0, h)),
            pl.BlockSpec((1, 1, DK_C), lambda b, h, i: (h, 0, 0)),
            pl.BlockSpec((1, DV_C), lambda b, h, i: (0, 0)),
            one(BLK_CQ), one(BLK_CK), two(BLK_CV), two(BLK_CR),
            pl.BlockSpec((1, 1, tc, LANE), lambda b, h, i: (BLK_CGD, b, i, 0)),
        ],
        out_specs=pl.BlockSpec((tc, DV_C), lambda b, h, i: (b * nt + i, h)),
        out_shape=jax.ShapeDtypeStruct((bsz * seq, MIX_WIDTH), BF16),
        scratch_shapes=[pltpu.VMEM((DV_C, DK_C), F32)],
        compiler_params=_params("parallel", "parallel", "arbitrary"),
        name="gla",
    )(w_up_p, b_up.reshape(H_C, 1, DK_C), out_gain.reshape(1, DV_C), u4, u4, u4, u4, u4)


def _merge_kernel(ya_ref, yb_ref, yc_ref, gl_ref, wb_ref, wo_ref, x_ref, gate_ref, o_ref, m_ref):
    n = pl.program_id(1)

    def contribution(y_ref):
        z = _dot(y_ref[...], wb_ref[0])
        cols = []
        for c in range(D_MODEL // LANE):
            cols.append(_sigmoid(gl_ref[c].astype(F32)) * z[:, c * LANE:(c + 1) * LANE])
        return jnp.concatenate(cols, axis=1)

    @pl.when(n == 0)
    def _():
        m_ref[...] = contribution(ya_ref)

    @pl.when(n == 1)
    def _():
        m_ref[...] += contribution(yb_ref)

    @pl.when(n == 2)
    def _():
        m = m_ref[...] + contribution(yc_ref)
        o_ref[...] = x_ref[...] + gate_ref[0] * _dot(m.astype(BF16), wo_ref[...])


def _merge(ya, yb, yc, u3, w_branch, w_out, x2, gate, seq):
    t = x2.shape[0]
    tm = min(512, seq)
    per_b = seq // tm
    ysp = pl.BlockSpec((tm, MIX_WIDTH), lambda i, n: (i, 0))
    row = pl.BlockSpec((tm, D_MODEL), lambda i, n: (i, 0))
    return pl.pallas_call(
        _merge_kernel,
        grid=(t // tm, N_BRANCH),
        in_specs=[
            ysp, ysp, ysp,
            pl.BlockSpec((D_MODEL // LANE, tm, LANE), lambda i, n: (n, i, 0)),
            pl.BlockSpec((1, MIX_WIDTH, D_MODEL), lambda i, n: (n, 0, 0)),
            pl.BlockSpec((D_MODEL, D_MODEL), lambda i, n: (0, 0)),
            row,
            pl.BlockSpec((1, 1, D_MODEL), lambda i, n: (i // per_b, 0, 0)),
        ],
        out_specs=row,
        out_shape=jax.ShapeDtypeStruct((t, D_MODEL), F32),
        scratch_shapes=[pltpu.VMEM((tm, D_MODEL), F32)],
        compiler_params=_params("parallel", "arbitrary"),
        name="merge_out",
    )(ya, yb, yc, u3, w_branch, w_out, x2, gate)


def kernel(x, c, w_ada, b_ada, norm_gains, ffn_w_gate, ffn_w_up, ffn_w_down, w_in, qk_gains, diff_lambda,
           diff_out_gain, rel_bias, hgrn_lb_logits, hgrn_out_gain, gla_w_gate_up, gla_b_gate, gla_out_gain,
           w_branch, w_out):
    bsz, seq, _ = x.shape
    t = bsz * seq
    tq = min(256, seq)
    tc = min(256, seq)

    lb_all = jnp.cumsum(jax.nn.softmax(hgrn_lb_logits.astype(F32), axis=0), axis=0)
    lb_all = lb_all - lb_all[0]
    mod = _ada(c, w_ada, b_ada).reshape(DEPTH, bsz, 3, 3, 1, D_MODEL)
    bias0, bias1 = _bias_tiles(rel_bias, tq)

    x2 = x.reshape(t, D_MODEL)
    for l in range(DEPTH):
        shift, scale, gate = mod[l, :, :, 0], mod[l, :, :, 1], mod[l, :, :, 2]
        gains = norm_gains[l].reshape(4, 1, D_MODEL)
        lam_init = 0.8 - 0.6 * math.exp(-0.3 * l)

        x2 = _ffn(x2, seq, shift[:, 0], scale[:, 0], gate[:, 0], gains[0], gains[3],
                  ffn_w_gate[l, 0].astype(BF16), ffn_w_up[l, 0].astype(BF16), ffn_w_down[l, 0].astype(BF16),
                  final_norm=False)

        w_l = w_in[l]
        w_in_p = jnp.concatenate(
            [w_l[:, N_MAIN + GLA_GATE_RANK:], w_l[:, :N_MAIN + GLA_GATE_RANK],
             jnp.zeros((D_MODEL, IN_TILE - GLA_GATE_RANK), w_l.dtype)], axis=1).astype(BF16)
        qk_tiles = jnp.stack([jnp.tile(qk_gains[l, 0] * (DH_A ** -0.5), 2)] * (H_A * LANE // IN_TILE)
                             + [jnp.tile(qk_gains[l, 1], 2)] * (H_A * LANE // IN_TILE)).reshape(-1, 1, LANE)
        u3 = _inproj(x2, seq, shift[:, 1], scale[:, 1], gains[1], w_in_p, qk_tiles)
        u4 = u3.reshape(N_BLK, bsz, seq, LANE)

        ya = _attention(u4, diff_lambda[l], diff_out_gain[l], bias0, bias1, lam_init=lam_init)
        yb = _hgrn(u4, lb_all[l], hgrn_out_gain[l], tc)
        w_up_p = jnp.concatenate(
            [gla_w_gate_up[l], jnp.zeros((LANE - GLA_GATE_RANK, H_C * DK_C), F32)], axis=0).astype(BF16)
        yc = _glac(u4, w_up_p, gla_b_gate[l], gla_out_gain[l], tc)

        x2 = _merge(ya, yb, yc, u3, w_branch[l].astype(BF16), w_out[l].astype(BF16), x2, gate[:, 1], seq)

        x2 = _ffn(x2, seq, shift[:, 2], scale[:, 2], gate[:, 2], gains[2], gains[3],
                  ffn_w_gate[l, 1].astype(BF16), ffn_w_up[l, 1].astype(BF16), ffn_w_down[l, 1].astype(BF16),
                  final_norm=True)
    return x2.reshape(bsz, seq, D_MODEL)
```

```python
import functools
import math

import jax
import jax.numpy as jnp
from jax import lax
from jax.experimental import pallas as pl
from jax.experimental.pallas import tpu as pltpu

D_MODEL = 2048
DEPTH = 2
CHUNK = 64
MIX_WIDTH = D_MODEL // 2
N_BRANCH = 3
H_A = 8
DH_A = 64
H_B = 8
DK_B = 128
DV_B = 128
H_C = 4
DK_C = 128
DV_C = 256
GLA_GATE_RANK = 16
GLA_GATE_NORM = 16.0
D_FF = 11 * D_MODEL // 4
N_BUCKETS = 32
MAX_DISTANCE = 128
EPS = 1e-6

LANE = 128
MASK_VALUE = -1e30
VMEM_LIMIT = 56 * 1024 * 1024

N_GATE_BLK = N_BRANCH * D_MODEL // LANE
BLK_AQ = N_GATE_BLK
BLK_AK = BLK_AQ + H_A
BLK_AV = BLK_AK + H_A
BLK_BQ = BLK_AV + H_A
BLK_BF = BLK_BQ + H_B
BLK_BI = BLK_BF + H_B
BLK_BG = BLK_BI + H_B
BLK_CQ = BLK_BG + H_B
BLK_CK = BLK_CQ + H_C
BLK_CV = BLK_CK + H_C
BLK_CR = BLK_CV + 2 * H_C
BLK_CGD = BLK_CR + 2 * H_C
N_MAIN = 10 * MIX_WIDTH
IN_STEP_BLOCKS = 12
N_BLK = 132
N_IN_PAD = N_BLK * LANE

BF16 = jnp.bfloat16
F32 = jnp.float32


def _params(*sem):
    return pltpu.CompilerParams(dimension_semantics=sem, vmem_limit_bytes=VMEM_LIMIT)


def _sigmoid(x):
    return 1.0 / (1.0 + jnp.exp(-x))


def _log_sigmoid(x):
    return jnp.minimum(x, 0.0) - jnp.log(1.0 + jnp.exp(-jnp.abs(x)))


def _rms(x):
    return x * lax.rsqrt(jnp.mean(x * x, axis=-1, keepdims=True) + EPS)


def _dot(a, b):
    return jnp.dot(a, b, preferred_element_type=F32)


def _dot_nt(a, b):
    return lax.dot_general(a, b, (((1,), (1,)), ((), ())), preferred_element_type=F32)


def _dot_tn(a, b):
    return lax.dot_general(a, b, (((0,), (0,)), ((), ())), preferred_element_type=F32)


def _ada_kernel(c_ref, w_ref, b_ref, o_ref):
    c = c_ref[...]
    cond = c * _sigmoid(c)
    c_hi = cond.astype(BF16)
    c_lo = (cond - c_hi.astype(F32)).astype(BF16)
    w = w_ref[0]
    w_hi = w.astype(BF16)
    w_lo = (w - w_hi.astype(F32)).astype(BF16)
    o_ref[0] = _dot(c_hi, w_hi) + (_dot(c_lo, w_hi) + _dot(c_hi, w_lo)) + b_ref[0]


def _ada(c, w_ada, b_ada):
    bsz = c.shape[0]
    n = w_ada.shape[-1]
    tn = 1024
    return pl.pallas_call(
        _ada_kernel,
        grid=(DEPTH, n // tn),
        in_specs=[
            pl.BlockSpec((bsz, D_MODEL), lambda l, j: (0, 0)),
            pl.BlockSpec((1, D_MODEL, tn), lambda l, j: (l, 0, j)),
            pl.BlockSpec((1, 1, tn), lambda l, j: (l, 0, j)),
        ],
        out_specs=pl.BlockSpec((1, bsz, tn), lambda l, j: (l, 0, j)),
        out_shape=jax.ShapeDtypeStruct((DEPTH, bsz, n), F32),
        compiler_params=_params("parallel", "parallel"),
        name="ada_mod",
    )(c, w_ada, b_ada.reshape(DEPTH, 1, n))


def _modulated(x, gain, scale, shift):
    return (_rms(x) * gain) * (1.0 + scale) + shift


def _ffn_kernel(x_ref, shift_ref, scale_ref, gate_ref, gain_ref, fgain_ref, wg_ref, wu_ref, wd_ref,
                o_ref, h_ref, *, final_norm):
    j = pl.program_id(1)

    @pl.when(j == 0)
    def _():
        h = _modulated(x_ref[...], gain_ref[...], scale_ref[0], shift_ref[0])
        h_ref[...] = h.astype(BF16)
        o_ref[...] = jnp.zeros_like(o_ref)

    h = h_ref[...]
    g = _dot(h, wg_ref[...])
    u = _dot(h, wu_ref[...])
    a = (g * _sigmoid(g) * u).astype(BF16)
    o_ref[...] += _dot(a, wd_ref[...])

    @pl.when(j == pl.num_programs(1) - 1)
    def _():
        y = x_ref[...] + 0.5 * gate_ref[0] * o_ref[...]
        if final_norm:
            y = _rms(y) * fgain_ref[...]
        o_ref[...] = y


def _ffn(x2, seq, shift, scale, gate, gain, fgain, wg, wu, wd, *, final_norm):
    t = x2.shape[0]
    tm = min(1024, seq)
    tf = 512
    per_b = seq // tm
    row = pl.BlockSpec((tm, D_MODEL), lambda i, j: (i, 0))
    row_once = pl.BlockSpec((tm, D_MODEL), lambda i, j: (i, 0), pipeline_mode=pl.Buffered(1))
    vec_b = pl.BlockSpec((1, 1, D_MODEL), lambda i, j: (i // per_b, 0, 0))
    vec = pl.BlockSpec((1, D_MODEL), lambda i, j: (0, 0))
    return pl.pallas_call(
        functools.partial(_ffn_kernel, final_norm=final_norm),
        grid=(t // tm, D_FF // tf),
        in_specs=[row_once, vec_b, vec_b, vec_b, vec, vec,
                  pl.BlockSpec((D_MODEL, tf), lambda i, j: (0, j)),
                  pl.BlockSpec((D_MODEL, tf), lambda i, j: (0, j)),
                  pl.BlockSpec((tf, D_MODEL), lambda i, j: (j, 0))],
        out_specs=row,
        out_shape=jax.ShapeDtypeStruct((t, D_MODEL), F32),
        scratch_shapes=[pltpu.VMEM((tm, D_MODEL), BF16)],
        compiler_params=_params("parallel", "arbitrary"),
        name="ffn",
    )(x2, shift, scale, gate, gain, fgain, wg, wu, wd)


def _inproj_kernel(x_ref, shift_ref, scale_ref, gain_ref, w_ref, o_ref, h_ref):
    @pl.when(pl.program_id(1) == 0)
    def _():
        h = _modulated(x_ref[...], gain_ref[...], scale_ref[0], shift_ref[0])
        h_ref[...] = h.astype(BF16)

    res = _dot(h_ref[...], w_ref[...])
    for s in range(IN_STEP_BLOCKS):
        o_ref[s] = res[:, s * LANE:(s + 1) * LANE].astype(o_ref.dtype)


def _inproj(x2, seq, shift, scale, gain, w_in_p):
    t = x2.shape[0]
    tm = min(1024, seq)
    per_b = seq // tm
    tn = IN_STEP_BLOCKS * LANE
    return pl.pallas_call(
        _inproj_kernel,
        grid=(t // tm, N_BLK // IN_STEP_BLOCKS),
        in_specs=[
            pl.BlockSpec((tm, D_MODEL), lambda i, j: (i, 0)),
            pl.BlockSpec((1, 1, D_MODEL), lambda i, j: (i // per_b, 0, 0)),
            pl.BlockSpec((1, 1, D_MODEL), lambda i, j: (i // per_b, 0, 0)),
            pl.BlockSpec((1, D_MODEL), lambda i, j: (0, 0)),
            pl.BlockSpec((D_MODEL, tn), lambda i, j: (0, j)),
        ],
        out_specs=pl.BlockSpec((IN_STEP_BLOCKS, tm, LANE), lambda i, j: (j, i, 0)),
        out_shape=jax.ShapeDtypeStruct((N_BLK, t, LANE), BF16),
        scratch_shapes=[pltpu.VMEM((tm, D_MODEL), BF16)],
        compiler_params=_params("parallel", "arbitrary"),
        name="in_proj",
    )(x2, shift, scale, gain, w_in_p)


def _t5_bucket(rel):
    half = N_BUCKETS // 2
    max_exact = half // 2
    ret = jnp.where(rel > 0, half, 0)
    n = jnp.abs(rel)
    nf = jnp.maximum(n, 1).astype(F32)
    large = max_exact + (jnp.log(nf / max_exact) / math.log(MAX_DISTANCE / max_exact)
                         * (half - max_exact)).astype(jnp.int32)
    large = jnp.minimum(large, half - 1)
    return ret + jnp.where(n < max_exact, n, large)


def _far_bucket_is_constant(min_dist):
    half = N_BUCKETS // 2
    max_exact = half // 2
    val = math.log(min_dist / max_exact) / math.log(MAX_DISTANCE / max_exact) * (half - max_exact)
    return max_exact + val >= half


def _bias_tiles(rel_bias, tq):
    assert _far_bucket_is_constant(tq + 1)
    kpos = jnp.arange(tq)[:, None]
    qpos = jnp.arange(tq)[None, :]
    table = rel_bias.astype(F32) - rel_bias[N_BUCKETS // 2 - 1].astype(F32)

    def lookup(bucket):
        out = jnp.zeros((H_A, tq, tq), F32)
        for n in range(N_BUCKETS):
            out = jnp.where(bucket[None] == n, table[n][:, None, None], out)
        return out

    b0 = lookup(_t5_bucket(kpos - qpos))
    b1 = lookup(_t5_bucket(kpos - tq - qpos))
    visible = (kpos // CHUNK) <= (qpos // CHUNK)
    return jnp.where(visible[None], b0, MASK_VALUE), b1


ONES_ROWS = 16


def _qk_norm(x, gain):
    low = lax.broadcasted_iota(jnp.int32, (1, LANE), 1) < DH_A
    sq = x * x
    lo = jnp.sum(jnp.where(low, sq, 0.0), axis=-1, keepdims=True)
    hi = jnp.sum(jnp.where(low, 0.0, sq), axis=-1, keepdims=True)
    ms = jnp.where(low, lo, hi) * (1.0 / DH_A)
    return x * lax.rsqrt(ms + EPS) * gain


def _attn_kernel(lv_ref, og_ref, qg_ref, kg_ref, q_ref, k_ref, v_ref, b0_ref, b1_ref, o_ref,
                 kn_ref, vt_ref, m_ref, acc_ref, *, tq, lam_init):
    qi = pl.program_id(2)
    seq = k_ref.shape[2]

    @pl.when(qi == 0)
    def _():
        for blk in range(seq // tq):
            rows = slice(blk * tq, (blk + 1) * tq)
            kn_ref[rows, :] = _qk_norm(k_ref[0, 0, rows, :].astype(F32), kg_ref[...]).astype(kn_ref.dtype)
            vt_ref[0:LANE, rows] = v_ref[0, 0, rows, :].astype(F32).T.astype(vt_ref.dtype)
        vt_ref[LANE:, :] = jnp.ones((ONES_ROWS, seq), vt_ref.dtype)

    qt = _qk_norm(q_ref[0, 0].astype(F32), qg_ref[...]).T
    sub = lax.broadcasted_iota(jnp.int32, (LANE, 1), 0)
    qst = jnp.concatenate([jnp.where(sub < DH_A, qt, 0.0), jnp.where(sub < DH_A, 0.0, qt)], axis=1).astype(BF16)

    m_ref[...] = jnp.full_like(m_ref, MASK_VALUE)
    acc_ref[...] = jnp.zeros_like(acc_ref)

    def update(kstart, bias):
        kb = kn_ref[pl.ds(kstart, tq), :]
        vtb = vt_ref[:, pl.ds(kstart, tq)]
        s = _dot(kb, qst)
        if bias is not None:
            s = s + jnp.concatenate([bias, bias], axis=1)
        m_prev = m_ref[0:1, :]
        m_new = jnp.maximum(m_prev, jnp.max(s, axis=0, keepdims=True))
        alpha = jnp.exp(m_prev - m_new)
        p = jnp.exp(s - m_new).astype(BF16)
        m_ref[...] = jnp.broadcast_to(m_new, m_ref.shape)
        acc_ref[...] = alpha * acc_ref[...] + _dot(vtb, p)

    update(pl.multiple_of(qi * tq, tq), b0_ref[0])

    @pl.when(qi >= 1)
    def _():
        update(pl.multiple_of((qi - 1) * tq, tq), b1_ref[0])

    def far(kj, carry):
        update(pl.multiple_of(kj * tq, tq), None)
        return carry

    lax.fori_loop(0, jnp.maximum(qi - 1, 0), far, 0)

    acc = acc_ref[...]
    ot = acc[0:LANE] / acc[LANE:LANE + 1]
    lv = lv_ref[...]
    lam = (jnp.exp(jnp.sum(lv[0:1] * lv[1:2], axis=-1, keepdims=True))
           - jnp.exp(jnp.sum(lv[2:3] * lv[3:4], axis=-1, keepdims=True)) + lam_init)
    odt = ot[:, :tq] - lam * ot[:, tq:]
    yt = odt * lax.rsqrt(jnp.mean(odt * odt, axis=0, keepdims=True) + EPS)
    o_ref[...] = (yt.T * og_ref[...] * (1.0 - lam_init)).astype(o_ref.dtype)


def _attention(u4, lam_vec, out_gain, qk_gain, bias0, bias1, *, lam_init):
    _, bsz, seq, _ = u4.shape
    tq = bias0.shape[-1]
    nq = seq // tq
    kv_spec = lambda base: pl.BlockSpec((1, 1, seq, LANE), lambda b, h, i: (base + h, b, 0, 0))
    bias_spec = pl.BlockSpec((1, tq, tq), lambda b, h, i: (h, 0, 0))
    vec = pl.BlockSpec((1, LANE), lambda b, h, i: (0, 0))
    q_gain = jnp.tile(qk_gain[0] * (DH_A ** -0.5), 2).reshape(1, LANE)
    k_gain = jnp.tile(qk_gain[1], 2).reshape(1, LANE)
    return pl.pallas_call(
        functools.partial(_attn_kernel, tq=tq, lam_init=lam_init),
        grid=(bsz, H_A, nq),
        in_specs=[
            pl.BlockSpec((4, DH_A), lambda b, h, i: (0, 0)),
            vec, vec, vec,
            pl.BlockSpec((1, 1, tq, LANE), lambda b, h, i: (BLK_AQ + h, b, i, 0)),
            kv_spec(BLK_AK), kv_spec(BLK_AV), bias_spec, bias_spec,
        ],
        out_specs=pl.BlockSpec((tq, LANE), lambda b, h, i: (b * nq + i, h)),
        out_shape=jax.ShapeDtypeStruct((bsz * seq, MIX_WIDTH), BF16),
        scratch_shapes=[pltpu.VMEM((seq, LANE), BF16),
                        pltpu.VMEM((LANE + ONES_ROWS, seq), BF16),
                        pltpu.VMEM((8, 2 * tq), F32),
                        pltpu.VMEM((LANE + ONES_ROWS, 2 * tq), F32)],
        compiler_params=_params("parallel", "parallel", "arbitrary"),
        name="diff_attn",
    )(lam_vec, out_gain.reshape(1, LANE), q_gain, k_gain, u4, u4, u4, bias0, bias1)


GLA_TILE = 2 * CHUNK


def _chunk_rows(x, rows, pick):
    r, d = x.shape
    g = x.reshape(r // rows, rows, d)[:, pick:pick + 1, :]
    return jnp.broadcast_to(g, (r // rows, rows, d)).reshape(r, d)


def _gla_block(q, k, v, g, st_ref):
    r_rows, dk = q.shape
    loc = lax.broadcasted_iota(jnp.int32, (r_rows, 1), 0) % CHUNK

    b = g
    sh = 1
    while sh < CHUNK:
        b = b + jnp.where(loc >= sh, pltpu.roll(b, sh, axis=0), 0.0)
        sh *= 2

    levels = []
    w = CHUNK // 2
    while w >= 1:
        odd = (loc // w) % 2 == 1
        if w >= 4:
            ref_b = _chunk_rows(b, 2 * w, w - 1)
            d = jnp.where(odd, b - ref_b, ref_b - b)
        elif w == 2:
            i4 = loc % 4
            g_next = pltpu.roll(g, r_rows - 1, axis=0)
            g_prev = pltpu.roll(g, 1, axis=0)
            d = jnp.where(i4 == 0, g_next, jnp.where(i4 == 2, g, jnp.where(i4 == 3, g + g_prev, 0.0)))
        else:
            d = jnp.where(odd, g, 0.0)
        levels.append((w, (jnp.where(odd, q, k) * jnp.exp(d)).astype(BF16)))
        w //= 2
    qb = q.astype(BF16)
    kb = k.astype(BF16)

    ti = lax.broadcasted_iota(jnp.int32, (GLA_TILE, 1), 0)
    si = lax.broadcasted_iota(jnp.int32, (1, GLA_TILE), 1)
    lev = jnp.where(si < ti, jnp.bitwise_xor(ti, si), 0)
    masks = [(lev >= w) & (lev < 2 * w) for w, _ in levels]
    diag = ti == si

    intra = []
    for sb in range(r_rows // GLA_TILE):
        sl = slice(sb * GLA_TILE, (sb + 1) * GLA_TILE)
        sc = jnp.where(diag, _dot_nt(qb[sl], kb[sl]), 0.0)
        for (w, x), mk in zip(levels, masks):
            sc = jnp.where(mk, _dot_nt(x[sl], x[sl]), sc)
        intra.append(_dot(sc.astype(BF16), v[sl]))

    b_last = _chunk_rows(b, CHUNK, CHUNK - 1)
    qe = (q * jnp.exp(b)).astype(BF16)
    ke = (k * jnp.exp(b_last - b)).astype(BF16)
    dec = jnp.exp(b_last)
    st = st_ref[...]
    inter = []
    for c in range(r_rows // CHUNK):
        sl = slice(c * CHUNK, (c + 1) * CHUNK)
        inter.append(_dot_nt(qe[sl], st.astype(BF16)))
        st = st * dec[c * CHUNK:c * CHUNK + 1] + _dot_tn(v[sl], ke[sl])
    st_ref[...] = st
    return jnp.concatenate(intra, axis=0) + jnp.concatenate(inter, axis=0)


def _hgrn_kernel(lb_ref, og_ref, q_ref, f_ref, i_ref, g_ref, o_ref, st_ref):
    @pl.when(pl.program_id(2) == 0)
    def _():
        st_ref[...] = jnp.zeros_like(st_ref)

    lb = lb_ref[0]
    zf = f_ref[0, 0].astype(F32)
    t = jnp.exp(-jnp.abs(zf))
    inv = 1.0 / (1.0 + t)
    pos = zf >= 0.0
    sig = jnp.where(pos, 1.0, t) * inv
    log_f = jnp.where(lb > 0.0, jnp.log(lb + (1.0 - lb) * sig), jnp.minimum(zf, 0.0) - jnp.log(1.0 + t))
    k = (1.0 - lb) * (jnp.where(pos, t, 1.0) * inv)
    o = _gla_block(q_ref[0, 0].astype(F32), k, i_ref[0, 0], log_f, st_ref)
    o = o * _sigmoid(g_ref[0, 0].astype(F32))
    o_ref[...] = (_rms(o) * og_ref[...]).astype(o_ref.dtype)


def _hgrn(u4, lb, out_gain, tc):
    _, bsz, seq, _ = u4.shape
    nt = seq // tc
    blk = lambda base: pl.BlockSpec((1, 1, tc, LANE), lambda b, h, i: (base + h, b, i, 0))
    return pl.pallas_call(
        _hgrn_kernel,
        grid=(bsz, H_B, nt),
        in_specs=[
            pl.BlockSpec((1, 1, DK_B), lambda b, h, i: (h, 0, 0)),
            pl.BlockSpec((1, DV_B), lambda b, h, i: (0, 0)),
            blk(BLK_BQ), blk(BLK_BF), blk(BLK_BI), blk(BLK_BG),
        ],
        out_specs=pl.BlockSpec((tc, DV_B), lambda b, h, i: (b * nt + i, h)),
        out_shape=jax.ShapeDtypeStruct((bsz * seq, MIX_WIDTH), BF16),
        scratch_shapes=[pltpu.VMEM((DV_B, DK_B), F32)],
        compiler_params=_params("parallel", "parallel", "arbitrary"),
        name="hgrn2",
    )(lb.reshape(H_B, 1, DK_B), out_gain.reshape(1, DV_B), u4, u4, u4, u4)


def _glac_kernel(wup_ref, bup_ref, og_ref, q_ref, k_ref, v_ref, r_ref, gd_ref, o_ref, st_ref):
    @pl.when(pl.program_id(2) == 0)
    def _():
        st_ref[...] = jnp.zeros_like(st_ref)

    z = _dot(gd_ref[0, 0], wup_ref[...]) + bup_ref[0]
    log_a = _log_sigmoid(z) * (1.0 / GLA_GATE_NORM)
    q = q_ref[0, 0].astype(F32) * (DK_C ** -0.5)
    v = jnp.concatenate([v_ref[0, 0], v_ref[1, 0]], axis=1)
    o = _gla_block(q, k_ref[0, 0].astype(F32), v, log_a, st_ref)
    gate = jnp.concatenate([r_ref[0, 0], r_ref[1, 0]], axis=1).astype(F32)
    o_ref[...] = (_rms(o) * og_ref[...] * (gate * _sigmoid(gate))).astype(o_ref.dtype)


def _glac(u4, w_up_p, b_up, out_gain, tc):
    _, bsz, seq, _ = u4.shape
    nt = seq // tc
    one = lambda base: pl.BlockSpec((1, 1, tc, LANE), lambda b, h, i: (base + h, b, i, 0))
    two = lambda base: pl.BlockSpec((2, 1, tc, LANE), lambda b, h, i: (base // 2 + h, b, i, 0))
    return pl.pallas_call(
        _glac_kernel,
        grid=(bsz, H_C, nt),
        in_specs=[
            pl.BlockSpec((LANE, DK_C), lambda b, h, i: (0, h)),
            pl.BlockSpec((1, 1, DK_C), lambda b, h, i: (h, 0, 0)),
            pl.BlockSpec((1, DV_C), lambda b, h, i: (0, 0)),
            one(BLK_CQ), one(BLK_CK), two(BLK_CV), two(BLK_CR),
            pl.BlockSpec((1, 1, tc, LANE), lambda b, h, i: (BLK_CGD, b, i, 0)),
        ],
        out_specs=pl.BlockSpec((tc, DV_C), lambda b, h, i: (b * nt + i, h)),
        out_shape=jax.ShapeDtypeStruct((bsz * seq, MIX_WIDTH), BF16),
        scratch_shapes=[pltpu.VMEM((DV_C, DK_C), F32)],
        compiler_params=_params("parallel", "parallel", "arbitrary"),
        name="gla",
    )(w_up_p, b_up.reshape(H_C, 1, DK_C), out_gain.reshape(1, DV_C), u4, u4, u4, u4, u4)


def _merge_kernel(ya_ref, yb_ref, yc_ref, gl_ref, wb_ref, wo_ref, x_ref, gate_ref, o_ref, m_ref):
    n = pl.program_id(1)

    def contribution(y_ref):
        z = _dot(y_ref[...], wb_ref[0])
        cols = []
        for c in range(D_MODEL // LANE):
            cols.append(_sigmoid(gl_ref[c].astype(F32)) * z[:, c * LANE:(c + 1) * LANE])
        return jnp.concatenate(cols, axis=1)

    @pl.when(n == 0)
    def _():
        m_ref[...] = contribution(ya_ref)

    @pl.when(n == 1)
    def _():
        m_ref[...] += contribution(yb_ref)

    @pl.when(n == 2)
    def _():
        m = m_ref[...] + contribution(yc_ref)
        o_ref[...] = x_ref[...] + gate_ref[0] * _dot(m.astype(BF16), wo_ref[...])


def _merge(ya, yb, yc, u3, w_branch, w_out, x2, gate, seq):
    t = x2.shape[0]
    tm = min(512, seq)
    per_b = seq // tm
    ysp = pl.BlockSpec((tm, MIX_WIDTH), lambda i, n: (i, 0))
    row = pl.BlockSpec((tm, D_MODEL), lambda i, n: (i, 0))
    return pl.pallas_call(
        _merge_kernel,
        grid=(t // tm, N_BRANCH),
        in_specs=[
            ysp, ysp, ysp,
            pl.BlockSpec((D_MODEL // LANE, tm, LANE), lambda i, n: (n, i, 0)),
            pl.BlockSpec((1, MIX_WIDTH, D_MODEL), lambda i, n: (n, 0, 0)),
            pl.BlockSpec((D_MODEL, D_MODEL), lambda i, n: (0, 0)),
            row,
            pl.BlockSpec((1, 1, D_MODEL), lambda i, n: (i // per_b, 0, 0)),
        ],
        out_specs=row,
        out_shape=jax.ShapeDtypeStruct((t, D_MODEL), F32),
        scratch_shapes=[pltpu.VMEM((tm, D_MODEL), F32)],
        compiler_params=_params("parallel", "arbitrary"),
        name="merge_out",
    )(ya, yb, yc, u3, w_branch, w_out, x2, gate)


def kernel(x, c, w_ada, b_ada, norm_gains, ffn_w_gate, ffn_w_up, ffn_w_down, w_in, qk_gains, diff_lambda,
           diff_out_gain, rel_bias, hgrn_lb_logits, hgrn_out_gain, gla_w_gate_up, gla_b_gate, gla_out_gain,
           w_branch, w_out):
    bsz, seq, _ = x.shape
    t = bsz * seq
    tq = min(512, seq)
    tc = min(512, seq)

    lb_all = jnp.cumsum(jax.nn.softmax(hgrn_lb_logits.astype(F32), axis=0), axis=0)
    lb_all = lb_all - lb_all[0]
    mod = _ada(c, w_ada, b_ada).reshape(DEPTH, bsz, 3, 3, 1, D_MODEL)
    bias0, bias1 = _bias_tiles(rel_bias, tq)

    x2 = x.reshape(t, D_MODEL)
    for l in range(DEPTH):
        shift, scale, gate = mod[l, :, :, 0], mod[l, :, :, 1], mod[l, :, :, 2]
        gains = norm_gains[l].reshape(4, 1, D_MODEL)
        lam_init = 0.8 - 0.6 * math.exp(-0.3 * l)

        x2 = _ffn(x2, seq, shift[:, 0], scale[:, 0], gate[:, 0], gains[0], gains[3],
                  ffn_w_gate[l, 0].astype(BF16), ffn_w_up[l, 0].astype(BF16), ffn_w_down[l, 0].astype(BF16),
                  final_norm=False)

        w_l = w_in[l]
        n_real = N_MAIN + GLA_GATE_RANK
        w_in_p = jnp.concatenate(
            [w_l[:, n_real:].astype(BF16), w_l[:, :n_real].astype(BF16),
             jnp.zeros((D_MODEL, N_IN_PAD - w_l.shape[1]), BF16)], axis=1)
        u3 = _inproj(x2, seq, shift[:, 1], scale[:, 1], gains[1], w_in_p)
        u4 = u3.reshape(N_BLK, bsz, seq, LANE)

        ya = _attention(u4, diff_lambda[l], diff_out_gain[l], qk_gains[l], bias0, bias1, lam_init=lam_init)
        yb = _hgrn(u4, lb_all[l], hgrn_out_gain[l], tc)
        w_up_p = jnp.concatenate(
            [gla_w_gate_up[l], jnp.zeros((LANE - GLA_GATE_RANK, H_C * DK_C), F32)], axis=0).astype(BF16)
        yc = _glac(u4, w_up_p, gla_b_gate[l], gla_out_gain[l], tc)

        x2 = _merge(ya, yb, yc, u3, w_branch[l].astype(BF16), w_out[l].astype(BF16), x2, gate[:, 1], seq)

        x2 = _ffn(x2, seq, shift[:, 2], scale[:, 2], gate[:, 2], gains[2], gains[3],
                  ffn_w_gate[l, 1].astype(BF16), ffn_w_up[l, 1].astype(BF16), ffn_w_down[l, 1].astype(BF16),
                  final_norm=True)
    return x2.reshape(bsz, seq, D_MODEL)
```

```python
import functools
import math

import jax
import jax.numpy as jnp
from jax import lax
from jax.experimental import pallas as pl
from jax.experimental.pallas import tpu as pltpu

D_MODEL = 2048
DEPTH = 2
CHUNK = 64
MIX_WIDTH = D_MODEL // 2
N_BRANCH = 3
H_A = 8
DH_A = 64
H_B = 8
DK_B = 128
DV_B = 128
H_C = 4
DK_C = 128
DV_C = 256
GLA_GATE_RANK = 16
GLA_GATE_NORM = 16.0
D_FF = 11 * D_MODEL // 4
N_BUCKETS = 32
MAX_DISTANCE = 128
EPS = 1e-6

LANE = 128
SUBLANES = 8
ROW_CHUNK = 256
MASK_VALUE = -1e30
VMEM_LIMIT = 56 * 1024 * 1024
N_GATE_BLK = N_BRANCH * D_MODEL // LANE
BLK_AQ = N_GATE_BLK
BLK_AK = BLK_AQ + H_A
BLK_AV = BLK_AK + H_A
BLK_BQ = BLK_AV + H_A
BLK_BF = BLK_BQ + H_B
BLK_BI = BLK_BF + H_B
BLK_BG = BLK_BI + H_B
BLK_CQ = BLK_BG + H_B
BLK_CK = BLK_CQ + H_C
BLK_CV = BLK_CK + H_C
BLK_CR = BLK_CV + 2 * H_C
BLK_CGD = BLK_CR + 2 * H_C
N_MAIN = 10 * MIX_WIDTH
IN_STEP_BLOCKS = 12
N_BLK = 132
N_IN_PAD = N_BLK * LANE

BF16 = jnp.bfloat16
F32 = jnp.float32


def _params(*sem):
    return pltpu.CompilerParams(dimension_semantics=sem, vmem_limit_bytes=VMEM_LIMIT)


def _sigmoid(x):
    return 1.0 / (1.0 + jnp.exp(-x))


def _log_sigmoid(x):
    return jnp.minimum(x, 0.0) - jnp.log(1.0 + jnp.exp(-jnp.abs(x)))


def _rms(x):
    return x * lax.rsqrt(jnp.mean(x * x, axis=-1, keepdims=True) + EPS)


def _dot(a, b):
    return jnp.dot(a, b, preferred_element_type=F32)


def _dot_nt(a, b):
    return lax.dot_general(a, b, (((1,), (1,)), ((), ())), preferred_element_type=F32)


def _dot_tn(a, b):
    return lax.dot_general(a, b, (((0,), (0,)), ((), ())), preferred_element_type=F32)


def _ada_kernel(c_ref, w_ref, b_ref, o_ref):
    c = c_ref[...]
    cond = c * _sigmoid(c)
    c_hi = cond.astype(BF16)
    c_lo = (cond - c_hi.astype(F32)).astype(BF16)
    w = w_ref[0]
    w_hi = w.astype(BF16)
    w_lo = (w - w_hi.astype(F32)).astype(BF16)
    o_ref[0] = _dot(c_hi, w_hi) + (_dot(c_lo, w_hi) + _dot(c_hi, w_lo)) + b_ref[0]


def _ada(c, w_ada, b_ada):
    bsz = c.shape[0]
    n = w_ada.shape[-1]
    tn = 1024
    return pl.pallas_call(
        _ada_kernel,
        grid=(DEPTH, n // tn),
        in_specs=[
            pl.BlockSpec((bsz, D_MODEL), lambda l, j: (0, 0)),
            pl.BlockSpec((1, D_MODEL, tn), lambda l, j: (l, 0, j)),
            pl.BlockSpec((1, 1, tn), lambda l, j: (l, 0, j)),
        ],
        out_specs=pl.BlockSpec((1, bsz, tn), lambda l, j: (l, 0, j)),
        out_shape=jax.ShapeDtypeStruct((DEPTH, bsz, n), F32),
        compiler_params=_params("parallel", "parallel"),
        name="ada_mod",
    )(c, w_ada, b_ada.reshape(DEPTH, 1, n))


def _modulated(x, gain, scale, shift):
    return (_rms(x) * gain) * (1.0 + scale) + shift


def _ffn_kernel(x_ref, shift_ref, scale_ref, gate_ref, gain_ref, fgain_ref, wg_ref, wu_ref, wd_ref,
                o_ref, h_ref, *, final_norm):
    j = pl.program_id(1)

    row_chunks = [slice(r, r + ROW_CHUNK) for r in range(0, x_ref.shape[0], ROW_CHUNK)]

    @pl.when(j == 0)
    def _():
        for rows in row_chunks:
            h = _modulated(x_ref[rows, :], gain_ref[...], scale_ref[0], shift_ref[0])
            h_ref[rows, :] = h.astype(BF16)
        o_ref[...] = jnp.zeros_like(o_ref)

    h = h_ref[...]
    g = _dot(h, wg_ref[...])
    u = _dot(h, wu_ref[...])
    a = (g * _sigmoid(g) * u).astype(BF16)
    o_ref[...] += _dot(a, wd_ref[...])

    @pl.when(j == pl.num_programs(1) - 1)
    def _():
        for rows in row_chunks:
            y = x_ref[rows, :] + 0.5 * gate_ref[0] * o_ref[rows, :]
            if final_norm:
                y = _rms(y) * fgain_ref[...]
            o_ref[rows, :] = y


def _ffn(x2, seq, shift, scale, gate, gain, fgain, wg, wu, wd, *, final_norm):
    t = x2.shape[0]
    tm = min(512, seq)
    tf = 512
    per_b = seq // tm
    row = pl.BlockSpec((tm, D_MODEL), lambda i, j: (i, 0))
    vec_b = pl.BlockSpec((1, 1, D_MODEL), lambda i, j: (i // per_b, 0, 0))
    vec = pl.BlockSpec((1, D_MODEL), lambda i, j: (0, 0))
    return pl.pallas_call(
        functools.partial(_ffn_kernel, final_norm=final_norm),
        grid=(t // tm, D_FF // tf),
        in_specs=[row, vec_b, vec_b, vec_b, vec, vec,
                  pl.BlockSpec((D_MODEL, tf), lambda i, j: (0, j)),
                  pl.BlockSpec((D_MODEL, tf), lambda i, j: (0, j)),
                  pl.BlockSpec((tf, D_MODEL), lambda i, j: (j, 0))],
        out_specs=row,
        out_shape=jax.ShapeDtypeStruct((t, D_MODEL), F32),
        scratch_shapes=[pltpu.VMEM((tm, D_MODEL), BF16)],
        compiler_params=_params("parallel", "arbitrary"),
        name="ffn",
    )(x2, shift, scale, gate, gain, fgain, wg, wu, wd)


def _inproj_kernel(x_ref, shift_ref, scale_ref, gain_ref, w_ref, o_ref, h_ref):
    @pl.when(pl.program_id(1) == 0)
    def _():
        for r in range(0, x_ref.shape[0], ROW_CHUNK):
            rows = slice(r, r + ROW_CHUNK)
            h = _modulated(x_ref[rows, :], gain_ref[...], scale_ref[0], shift_ref[0])
            h_ref[rows, :] = h.astype(BF16)

    res = _dot(h_ref[...], w_ref[...])
    for s in range(IN_STEP_BLOCKS):
        o_ref[s] = res[:, s * LANE:(s + 1) * LANE].astype(o_ref.dtype)


def _inproj(x2, seq, shift, scale, gain, w_in_p):
    t = x2.shape[0]
    tm = min(1024, seq)
    per_b = seq // tm
    tn = IN_STEP_BLOCKS * LANE
    return pl.pallas_call(
        _inproj_kernel,
        grid=(t // tm, N_BLK // IN_STEP_BLOCKS),
        in_specs=[
            pl.BlockSpec((tm, D_MODEL), lambda i, j: (i, 0)),
            pl.BlockSpec((1, 1, D_MODEL), lambda i, j: (i // per_b, 0, 0)),
            pl.BlockSpec((1, 1, D_MODEL), lambda i, j: (i // per_b, 0, 0)),
            pl.BlockSpec((1, D_MODEL), lambda i, j: (0, 0)),
            pl.BlockSpec((D_MODEL, tn), lambda i, j: (0, j)),
        ],
        out_specs=pl.BlockSpec((IN_STEP_BLOCKS, tm, LANE), lambda i, j: (j, i, 0)),
        out_shape=jax.ShapeDtypeStruct((N_BLK, t, LANE), BF16),
        scratch_shapes=[pltpu.VMEM((tm, D_MODEL), BF16)],
        compiler_params=_params("parallel", "arbitrary"),
        name="in_proj",
    )(x2, shift, scale, gain, w_in_p)


def _t5_bucket(rel):
    half = N_BUCKETS // 2
    max_exact = half // 2
    ret = jnp.where(rel > 0, half, 0)
    n = jnp.abs(rel)
    nf = jnp.maximum(n, 1).astype(F32)
    large = max_exact + (jnp.log(nf / max_exact) / math.log(MAX_DISTANCE / max_exact)
                         * (half - max_exact)).astype(jnp.int32)
    large = jnp.minimum(large, half - 1)
    return ret + jnp.where(n < max_exact, n, large)


def _far_bucket_is_constant(min_dist):
    half = N_BUCKETS // 2
    max_exact = half // 2
    val = math.log(min_dist / max_exact) / math.log(MAX_DISTANCE / max_exact) * (half - max_exact)
    return max_exact + val >= half


def _bias_tiles(rel_bias, tq):
    assert _far_bucket_is_constant(tq + 1)
    kpos = jnp.arange(tq)[:, None]
    qpos = jnp.arange(tq)[None, :]
    table = rel_bias.astype(F32) - rel_bias[N_BUCKETS // 2 - 1].astype(F32)

    def lookup(bucket):
        out = jnp.zeros((H_A, tq, tq), F32)
        for n in range(N_BUCKETS):
            out = jnp.where(bucket[None] == n, table[n][:, None, None], out)
        return out

    b0 = lookup(_t5_bucket(kpos - qpos))
    b1 = lookup(_t5_bucket(kpos - tq - qpos))
    visible = (kpos // CHUNK) <= (qpos // CHUNK)
    return jnp.where(visible[None], b0, MASK_VALUE), b1


ONES_ROWS = 16


def _qk_norm(x, gain):
    low = lax.broadcasted_iota(jnp.int32, (1, LANE), 1) < DH_A
    sq = x * x
    lo = jnp.sum(jnp.where(low, sq, 0.0), axis=-1, keepdims=True)
    hi = jnp.sum(jnp.where(low, 0.0, sq), axis=-1, keepdims=True)
    ms = jnp.where(low, lo, hi) * (1.0 / DH_A)
    return x * lax.rsqrt(ms + EPS) * gain


def _attn_kernel(lv_ref, og_ref, qg_ref, kg_ref, q_ref, k_ref, v_ref, b0_ref, b1_ref, o_ref,
                 kn_ref, vt_ref, qst_ref, sa_ref, sb_ref, m_ref, acc_ref, *, tq, lam_init):
    qi = pl.program_id(2)
    seq = k_ref.shape[2]

    @pl.when(qi == 0)
    def _():
        for blk in range(seq // tq):
            rows = slice(blk * tq, (blk + 1) * tq)
            kn_ref[rows, :] = _qk_norm(k_ref[0, 0, rows, :].astype(F32), kg_ref[...]).astype(kn_ref.dtype)
            vt_ref[0:LANE, rows] = v_ref[0, 0, rows, :].astype(F32).T.astype(vt_ref.dtype)
        vt_ref[LANE:, :] = jnp.ones((ONES_ROWS, seq), vt_ref.dtype)

    qt = _qk_norm(q_ref[0, 0].astype(F32), qg_ref[...]).T
    sub = lax.broadcasted_iota(jnp.int32, (LANE, 1), 0)
    qst_ref[...] = jnp.concatenate(
        [jnp.where(sub < DH_A, qt, 0.0), jnp.where(sub < DH_A, 0.0, qt)], axis=1).astype(qst_ref.dtype)

    m_ref[...] = jnp.full_like(m_ref, MASK_VALUE)
    acc_ref[...] = jnp.zeros_like(acc_ref)

    def logits(blk, bias, dst_ref):
        s = _dot(kn_ref[pl.ds(pl.multiple_of(blk * tq, tq), tq), :], qst_ref[...])
        if bias is not None:
            s = s + jnp.concatenate([bias, bias], axis=1)
        dst_ref[...] = s

    def accumulate(src_ref, blk):
        s = src_ref[...]
        vtb = vt_ref[:, pl.ds(pl.multiple_of(blk * tq, tq), tq)]
        m_prev = m_ref[0:1, :]
        m_new = jnp.maximum(m_prev, jnp.max(s, axis=0, keepdims=True))
        alpha = jnp.exp(m_prev - m_new)
        p = jnp.exp(s - m_new).astype(BF16)
        m_ref[...] = jnp.broadcast_to(m_new, m_ref.shape)
        acc_ref[...] = alpha * acc_ref[...] + _dot(vtb, p)

    n_far = jnp.maximum(qi - 1, 0)

    @pl.when(n_far > 0)
    def _():
        logits(0, None, sa_ref)

    def far_pair(i, carry):
        blk = 2 * i
        logits(blk + 1, None, sb_ref)
        accumulate(sa_ref, blk)
        logits(jnp.minimum(blk + 2, n_far - 1), None, sa_ref)
        accumulate(sb_ref, blk + 1)
        return carry

    lax.fori_loop(0, n_far // 2, far_pair, 0)

    @pl.when(n_far % 2 == 1)
    def _():
        accumulate(sa_ref, n_far - 1)

    @pl.when(qi >= 1)
    def _():
        logits(qi - 1, b1_ref[0], sa_ref)
        logits(qi, b0_ref[0], sb_ref)
        accumulate(sa_ref, qi - 1)
        accumulate(sb_ref, qi)

    @pl.when(qi == 0)
    def _():
        logits(0, b0_ref[0], sb_ref)
        accumulate(sb_ref, 0)

    acc = acc_ref[...]
    ot = acc[0:LANE] / acc[LANE:LANE + 1]
    lv = lv_ref[...]
    lam = (jnp.exp(jnp.sum(lv[0:1] * lv[1:2], axis=-1, keepdims=True))
           - jnp.exp(jnp.sum(lv[2:3] * lv[3:4], axis=-1, keepdims=True)) + lam_init)
    odt = ot[:, :tq] - lam * ot[:, tq:]
    yt = odt * lax.rsqrt(jnp.mean(odt * odt, axis=0, keepdims=True) + EPS)
    o_ref[...] = (yt.T * og_ref[...] * (1.0 - lam_init)).astype(o_ref.dtype)


def _attention(u4, lam_vec, out_gain, qk_gain, bias0, bias1, *, lam_init):
    _, bsz, seq, _ = u4.shape
    tq = bias0.shape[-1]
    nq = seq // tq
    kv_spec = lambda base: pl.BlockSpec((1, 1, seq, LANE), lambda b, h, i: (base + h, b, 0, 0))
    bias_spec = pl.BlockSpec((1, tq, tq), lambda b, h, i: (h, 0, 0))
    vec = pl.BlockSpec((1, LANE), lambda b, h, i: (0, 0))
    q_gain = jnp.tile(qk_gain[0] * (DH_A ** -0.5), 2).reshape(1, LANE)
    k_gain = jnp.tile(qk_gain[1], 2).reshape(1, LANE)
    return pl.pallas_call(
        functools.partial(_attn_kernel, tq=tq, lam_init=lam_init),
        grid=(bsz, H_A, nq),
        in_specs=[
            pl.BlockSpec((4, DH_A), lambda b, h, i: (0, 0)),
            vec, vec, vec,
            pl.BlockSpec((1, 1, tq, LANE), lambda b, h, i: (BLK_AQ + h, b, i, 0)),
            kv_spec(BLK_AK), kv_spec(BLK_AV), bias_spec, bias_spec,
        ],
        out_specs=pl.BlockSpec((tq, LANE), lambda b, h, i: (b * nq + i, h)),
        out_shape=jax.ShapeDtypeStruct((bsz * seq, MIX_WIDTH), BF16),
        scratch_shapes=[pltpu.VMEM((seq, LANE), BF16),
                        pltpu.VMEM((LANE + ONES_ROWS, seq), BF16),
                        pltpu.VMEM((LANE, 2 * tq), BF16),
                        pltpu.VMEM((tq, 2 * tq), F32),
                        pltpu.VMEM((tq, 2 * tq), F32),
                        pltpu.VMEM((8, 2 * tq), F32),
                        pltpu.VMEM((LANE + ONES_ROWS, 2 * tq), F32)],
        compiler_params=_params("parallel", "parallel", "arbitrary"),
        name="diff_attn",
    )(lam_vec, out_gain.reshape(1, LANE), q_gain, k_gain, u4, u4, u4, bias0, bias1)


GLA_TILE = 2 * CHUNK


def _gla_block(q, k, v, g, st_ref):
    r_rows, dk = q.shape
    n8 = r_rows // SUBLANES
    n_chunks = r_rows // CHUNK
    per_chunk = CHUNK // SUBLANES
    sub = lax.broadcasted_iota(jnp.int32, (1, SUBLANES, dk), 1)
    q3, k3, g3 = (a.reshape(n8, SUBLANES, dk) for a in (q, k, g))

    p = g3
    sh = 1
    while sh < SUBLANES:
        p = p + jnp.where(sub >= sh, pltpu.roll(p, sh, axis=1), 0.0)
        sh *= 2
    p4 = p.reshape(n_chunks, per_chunk, SUBLANES, dk)
    total = p4[:, 0, SUBLANES - 1:, :]
    groups = [p4[:, 0]]
    for j in range(1, per_chunk):
        groups.append(p4[:, j] + total)
        total = total + p4[:, j, SUBLANES - 1:, :]
    b4 = jnp.stack(groups, axis=1)
    b3 = b4.reshape(n8, SUBLANES, dk)

    levels = []
    w = CHUNK // 2
    while w >= SUBLANES:
        m = w // SUBLANES
        pairs = (n8 // (2 * m), 2, m, SUBLANES, dk)
        b5, q5, k5 = b3.reshape(pairs), q3.reshape(pairs), k3.reshape(pairs)
        ref = b5[:, 0:1, m - 1:, SUBLANES - 1:, :]
        d = jnp.concatenate([ref - b5[:, 0:1], b5[:, 1:2] - ref], axis=1)
        x = jnp.concatenate([k5[:, 0:1], q5[:, 1:2]], axis=1) * jnp.exp(d)
        levels.append((w, x.reshape(r_rows, dk).astype(BF16)))
        w //= 2
    while w >= 1:
        odd = (sub & w) != 0
        if w == 4:
            ref = b3[:, 3:4, :]
            d = jnp.where(odd, b3 - ref, ref - b3)
        elif w == 2:
            ref = jnp.where(sub < 4, b3[:, 1:2, :], b3[:, 5:6, :])
            d = jnp.where(odd, b3 - ref, ref - b3)
        else:
            d = jnp.where(odd, g3, 0.0)
        x = jnp.where(odd, q3, k3) * jnp.exp(d)
        levels.append((w, x.reshape(r_rows, dk).astype(BF16)))
        w //= 2
    qb = q.astype(BF16)
    kb = k.astype(BF16)

    ti = lax.broadcasted_iota(jnp.int32, (GLA_TILE, 1), 0)
    si = lax.broadcasted_iota(jnp.int32, (1, GLA_TILE), 1)
    lev = jnp.where(si < ti, jnp.bitwise_xor(ti, si), 0)
    masks = [(lev >= w) & (lev < 2 * w) for w, _ in levels]
    diag = ti == si

    intra = []
    for sb in range(r_rows // GLA_TILE):
        sl = slice(sb * GLA_TILE, (sb + 1) * GLA_TILE)
        sc = jnp.where(diag, _dot_nt(qb[sl], kb[sl]), 0.0)
        for (w, x), mk in zip(levels, masks):
            sc = jnp.where(mk, _dot_nt(x[sl], x[sl]), sc)
        intra.append(_dot(sc.astype(BF16), v[sl]))

    chunked = (n_chunks, per_chunk, SUBLANES, dk)
    b_last = total[:, None]
    qe = (q3.reshape(chunked) * jnp.exp(b4)).reshape(r_rows, dk).astype(BF16)
    ke = (k3.reshape(chunked) * jnp.exp(b_last - b4)).reshape(r_rows, dk).astype(BF16)
    dec = jnp.exp(total)
    st = st_ref[...]
    inter = []
    for c in range(n_chunks):
        sl = slice(c * CHUNK, (c + 1) * CHUNK)
        inter.append(_dot_nt(qe[sl], st.astype(BF16)))
        st = st * dec[c] + _dot_tn(v[sl], ke[sl])
    st_ref[...] = st
    return jnp.concatenate(intra, axis=0) + jnp.concatenate(inter, axis=0)


def _hgrn_kernel(lb_ref, og_ref, q_ref, f_ref, i_ref, g_ref, o_ref, st_ref):
    @pl.when(pl.program_id(2) == 0)
    def _():
        st_ref[...] = jnp.zeros_like(st_ref)

    lb = lb_ref[0]
    zf = f_ref[0, 0].astype(F32)
    t = jnp.exp(-jnp.abs(zf))
    inv = 1.0 / (1.0 + t)
    pos = zf >= 0.0
    sig = jnp.where(pos, 1.0, t) * inv
    log_f = jnp.where(lb > 0.0, jnp.log(lb + (1.0 - lb) * sig), jnp.minimum(zf, 0.0) - jnp.log(1.0 + t))
    k = (1.0 - lb) * (jnp.where(pos, t, 1.0) * inv)
    o = _gla_block(q_ref[0, 0].astype(F32), k, i_ref[0, 0], log_f, st_ref)
    o = o * _sigmoid(g_ref[0, 0].astype(F32))
    o_ref[...] = (_rms(o) * og_ref[...]).astype(o_ref.dtype)


def _hgrn(u4, lb, out_gain, tc):
    _, bsz, seq, _ = u4.shape
    nt = seq // tc
    blk = lambda base: pl.BlockSpec((1, 1, tc, LANE), lambda b, h, i: (base + h, b, i, 0))
    return pl.pallas_call(
        _hgrn_kernel,
        grid=(bsz, H_B, nt),
        in_specs=[
            pl.BlockSpec((1, 1, DK_B), lambda b, h, i: (h, 0, 0)),
            pl.BlockSpec((1, DV_B), lambda b, h, i: (0, 0)),
            blk(BLK_BQ), blk(BLK_BF), blk(BLK_BI), blk(BLK_BG),
        ],
        out_specs=pl.BlockSpec((tc, DV_B), lambda b, h, i: (b * nt + i, h)),
        out_shape=jax.ShapeDtypeStruct((bsz * seq, MIX_WIDTH), BF16),
        scratch_shapes=[pltpu.VMEM((DV_B, DK_B), F32)],
        compiler_params=_params("parallel", "parallel", "arbitrary"),
        name="hgrn2",
    )(lb.reshape(H_B, 1, DK_B), out_gain.reshape(1, DV_B), u4, u4, u4, u4)


def _glac_kernel(wup_ref, bup_ref, og_ref, q_ref, k_ref, v_ref, r_ref, gd_ref, o_ref, st_ref):
    @pl.when(pl.program_id(2) == 0)
    def _():
        st_ref[...] = jnp.zeros_like(st_ref)

    z = _dot(gd_ref[0, 0], wup_ref[...]) + bup_ref[0]
    log_a = _log_sigmoid(z) * (1.0 / GLA_GATE_NORM)
    q = q_ref[0, 0].astype(F32) * (DK_C ** -0.5)
    v = jnp.concatenate([v_ref[0, 0], v_ref[1, 0]], axis=1)
    o = _gla_block(q, k_ref[0, 0].astype(F32), v, log_a, st_ref)
    gate = jnp.concatenate([r_ref[0, 0], r_ref[1, 0]], axis=1).astype(F32)
    o_ref[...] = (_rms(o) * og_ref[...] * (gate * _sigmoid(gate))).astype(o_ref.dtype)


def _glac(u4, w_up_p, b_up, out_gain, tc):
    _, bsz, seq, _ = u4.shape
    nt = seq // tc
    one = lambda base: pl.BlockSpec((1, 1, tc, LANE), lambda b, h, i: (base + h, b, i, 0))
    two = lambda base: pl.BlockSpec((2, 1, tc, LANE), lambda b, h, i: (base // 2 + h, b, i, 0))
    return pl.pallas_call(
        _glac_kernel,
        grid=(bsz, H_C, nt),
        in_specs=[
            pl.BlockSpec((LANE, DK_C), lambda b, h, i: (0, h)),
            pl.BlockSpec((1, 1, DK_C), lambda b, h, i: (h, 0, 0)),
            pl.BlockSpec((1, DV_C), lambda b, h, i: (0, 0)),
            one(BLK_CQ), one(BLK_CK), two(BLK_CV), two(BLK_CR),
            pl.BlockSpec((1, 1, tc, LANE), lambda b, h, i: (BLK_CGD, b, i, 0)),
        ],
        out_specs=pl.BlockSpec((tc, DV_C), lambda b, h, i: (b * nt + i, h)),
        out_shape=jax.ShapeDtypeStruct((bsz * seq, MIX_WIDTH), BF16),
        scratch_shapes=[pltpu.VMEM((DV_C, DK_C), F32)],
        compiler_params=_params("parallel", "parallel", "arbitrary"),
        name="gla",
    )(w_up_p, b_up.reshape(H_C, 1, DK_C), out_gain.reshape(1, DV_C), u4, u4, u4, u4, u4)


def _merge_kernel(ya_ref, yb_ref, yc_ref, gl_ref, wb_ref, wo_ref, x_ref, gate_ref, o_ref):
    n = pl.program_id(1)
    m_ref = o_ref

    def contribution(y_ref, branch):
        z = _dot(y_ref[...], wb_ref[branch])
        cols = []
        for c in range(D_MODEL // LANE):
            cols.append(_sigmoid(gl_ref[c].astype(F32)) * z[:, c * LANE:(c + 1) * LANE])
        return jnp.concatenate(cols, axis=1)

    @pl.when(n == 0)
    def _():
        m_ref[...] = contribution(ya_ref, 0)

    @pl.when(n == 1)
    def _():
        m_ref[...] += contribution(yb_ref, 1)

    @pl.when(n == 2)
    def _():
        m = m_ref[...] + contribution(yc_ref, 2)
        o_ref[...] = x_ref[...] + gate_ref[0] * _dot(m.astype(BF16), wo_ref[...])


def _merge(ya, yb, yc, u3, w_branch, w_out, x2, gate, seq):
    t = x2.shape[0]
    tm = min(512, seq)
    per_b = seq // tm
    ysp = pl.BlockSpec((tm, MIX_WIDTH), lambda i, n: (i, 0))
    row = pl.BlockSpec((tm, D_MODEL), lambda i, n: (i, 0))
    return pl.pallas_call(
        _merge_kernel,
        grid=(t // tm, N_BRANCH),
        in_specs=[
            ysp, ysp, ysp,
            pl.BlockSpec((D_MODEL // LANE, tm, LANE), lambda i, n: (n, i, 0)),
            pl.BlockSpec((N_BRANCH, MIX_WIDTH, D_MODEL), lambda i, n: (0, 0, 0), pipeline_mode=pl.Buffered(1)),
            pl.BlockSpec((D_MODEL, D_MODEL), lambda i, n: (0, 0), pipeline_mode=pl.Buffered(1)),
            row,
            pl.BlockSpec((1, 1, D_MODEL), lambda i, n: (i // per_b, 0, 0)),
        ],
        out_specs=row,
        out_shape=jax.ShapeDtypeStruct((t, D_MODEL), F32),
        compiler_params=_params("parallel", "arbitrary"),
        name="merge_out",
    )(ya, yb, yc, u3, w_branch, w_out, x2, gate)


def kernel(x, c, w_ada, b_ada, norm_gains, ffn_w_gate, ffn_w_up, ffn_w_down, w_in, qk_gains, diff_lambda,
           diff_out_gain, rel_bias, hgrn_lb_logits, hgrn_out_gain, gla_w_gate_up, gla_b_gate, gla_out_gain,
           w_branch, w_out):
    bsz, seq, _ = x.shape
    t = bsz * seq
    tq = min(512, seq)
    tc = min(512, seq)

    lb_all = jnp.cumsum(jax.nn.softmax(hgrn_lb_logits.astype(F32), axis=0), axis=0)
    lb_all = lb_all - lb_all[0]
    mod = _ada(c, w_ada, b_ada).reshape(DEPTH, bsz, 3, 3, 1, D_MODEL)
    bias0, bias1 = _bias_tiles(rel_bias, tq)

    x2 = x.reshape(t, D_MODEL)
    for l in range(DEPTH):
        shift, scale, gate = mod[l, :, :, 0], mod[l, :, :, 1], mod[l, :, :, 2]
        gains = norm_gains[l].reshape(4, 1, D_MODEL)
        lam_init = 0.8 - 0.6 * math.exp(-0.3 * l)

        x2 = _ffn(x2, seq, shift[:, 0], scale[:, 0], gate[:, 0], gains[0], gains[3],
                  ffn_w_gate[l, 0].astype(BF16), ffn_w_up[l, 0].astype(BF16), ffn_w_down[l, 0].astype(BF16),
                  final_norm=False)

        w_l = w_in[l]
        n_real = N_MAIN + GLA_GATE_RANK
        w_in_p = jnp.concatenate(
            [w_l[:, n_real:].astype(BF16), w_l[:, :n_real].astype(BF16),
             jnp.zeros((D_MODEL, N_IN_PAD - w_l.shape[1]), BF16)], axis=1)
        u3 = _inproj(x2, seq, shift[:, 1], scale[:, 1], gains[1], w_in_p)
        u4 = u3.reshape(N_BLK, bsz, seq, LANE)

        ya = _attention(u4, diff_lambda[l], diff_out_gain[l], qk_gains[l], bias0, bias1, lam_init=lam_init)
        yb = _hgrn(u4, lb_all[l], hgrn_out_gain[l], tc)
        w_up_p = jnp.concatenate(
            [gla_w_gate_up[l], jnp.zeros((LANE - GLA_GATE_RANK, H_C * DK_C), F32)], axis=0).astype(BF16)
        yc = _glac(u4, w_up_p, gla_b_gate[l], gla_out_gain[l], tc)

        x2 = _merge(ya, yb, yc, u3, w_branch[l].astype(BF16), w_out[l].astype(BF16), x2, gate[:, 1], seq)

        x2 = _ffn(x2, seq, shift[:, 2], scale[:, 2], gate[:, 2], gains[2], gains[3],
                  ffn_w_gate[l, 1].astype(BF16), ffn_w_up[l, 1].astype(BF16), ffn_w_down[l, 1].astype(BF16),
                  final_norm=True)
    return x2.reshape(bsz, seq, D_MODEL)
```

```python
import functools
import math

import jax
import jax.numpy as jnp
from jax import lax
from jax.experimental import pallas as pl
from jax.experimental.pallas import tpu as pltpu

D_MODEL = 2048
DEPTH = 2
CHUNK = 64
MIX_WIDTH = D_MODEL // 2
N_BRANCH = 3
H_A = 8
DH_A = 64
H_B = 8
DK_B = 128
DV_B = 128
H_C = 4
DK_C = 128
DV_C = 256
GLA_GATE_RANK = 16
GLA_GATE_NORM = 16.0
D_FF = 11 * D_MODEL // 4
N_BUCKETS = 32
MAX_DISTANCE = 128
EPS = 1e-6

LANE = 128
SUBLANES = 8
ROW_CHUNK = 256
MASK_VALUE = -1e30
VMEM_LIMIT = 56 * 1024 * 1024
N_GATE_BLK = N_BRANCH * D_MODEL // LANE
BLK_AQ = N_GATE_BLK
BLK_AK = BLK_AQ + H_A
BLK_AV = BLK_AK + H_A
BLK_BQ = BLK_AV + H_A
BLK_BF = BLK_BQ + H_B
BLK_BI = BLK_BF + H_B
BLK_BG = BLK_BI + H_B
BLK_CQ = BLK_BG + H_B
BLK_CK = BLK_CQ + H_C
BLK_CV = BLK_CK + H_C
BLK_CR = BLK_CV + 2 * H_C
BLK_CGD = BLK_CR + 2 * H_C
N_MAIN = 10 * MIX_WIDTH
IN_STEP_BLOCKS = 12
N_BLK = 132
N_IN_PAD = N_BLK * LANE

BF16 = jnp.bfloat16
F32 = jnp.float32


def _params(*sem):
    return pltpu.CompilerParams(dimension_semantics=sem, vmem_limit_bytes=VMEM_LIMIT)


def _sigmoid(x):
    return 1.0 / (1.0 + jnp.exp(-x))


def _log_sigmoid(x):
    return jnp.minimum(x, 0.0) - jnp.log(1.0 + jnp.exp(-jnp.abs(x)))


def _rms(x):
    return x * lax.rsqrt(jnp.mean(x * x, axis=-1, keepdims=True) + EPS)


def _dot(a, b):
    return jnp.dot(a, b, preferred_element_type=F32)


def _dot_nt(a, b):
    return lax.dot_general(a, b, (((1,), (1,)), ((), ())), preferred_element_type=F32)


def _dot_tn(a, b):
    return lax.dot_general(a, b, (((0,), (0,)), ((), ())), preferred_element_type=F32)


def _ada_kernel(c_ref, w_ref, b_ref, o_ref):
    c = c_ref[...]
    cond = c * _sigmoid(c)
    c_hi = cond.astype(BF16)
    c_lo = (cond - c_hi.astype(F32)).astype(BF16)
    w = w_ref[0]
    w_hi = w.astype(BF16)
    w_lo = (w - w_hi.astype(F32)).astype(BF16)
    o_ref[0] = _dot(c_hi, w_hi) + (_dot(c_lo, w_hi) + _dot(c_hi, w_lo)) + b_ref[0]


def _ada(c, w_ada, b_ada):
    bsz = c.shape[0]
    n = w_ada.shape[-1]
    tn = 1024
    return pl.pallas_call(
        _ada_kernel,
        grid=(DEPTH, n // tn),
        in_specs=[
            pl.BlockSpec((bsz, D_MODEL), lambda l, j: (0, 0)),
            pl.BlockSpec((1, D_MODEL, tn), lambda l, j: (l, 0, j)),
            pl.BlockSpec((1, 1, tn), lambda l, j: (l, 0, j)),
        ],
        out_specs=pl.BlockSpec((1, bsz, tn), lambda l, j: (l, 0, j)),
        out_shape=jax.ShapeDtypeStruct((DEPTH, bsz, n), F32),
        compiler_params=_params("parallel", "parallel"),
        name="ada_mod",
    )(c, w_ada, b_ada.reshape(DEPTH, 1, n))


def _modulated(x, gain, scale, shift):
    return (_rms(x) * gain) * (1.0 + scale) + shift


def _ffn_kernel(x_ref, shift_ref, scale_ref, gate_ref, gain_ref, fgain_ref, wg_ref, wu_ref, wd_ref,
                o_ref, h_ref, *, final_norm):
    j = pl.program_id(1)

    row_chunks = [slice(r, r + ROW_CHUNK) for r in range(0, x_ref.shape[0], ROW_CHUNK)]

    @pl.when(j == 0)
    def _():
        for rows in row_chunks:
            h = _modulated(x_ref[rows, :], gain_ref[...], scale_ref[0], shift_ref[0])
            h_ref[rows, :] = h.astype(BF16)
        o_ref[...] = jnp.zeros_like(o_ref)

    h = h_ref[...]
    g = _dot(h, wg_ref[...])
    u = _dot(h, wu_ref[...])
    a = (g * _sigmoid(g) * u).astype(BF16)
    o_ref[...] += _dot(a, wd_ref[...])

    @pl.when(j == pl.num_programs(1) - 1)
    def _():
        for rows in row_chunks:
            y = x_ref[rows, :] + 0.5 * gate_ref[0] * o_ref[rows, :]
            if final_norm:
                y = _rms(y) * fgain_ref[...]
            o_ref[rows, :] = y


def _ffn(x2, seq, shift, scale, gate, gain, fgain, wg, wu, wd, *, final_norm):
    t = x2.shape[0]
    tm = min(512, seq)
    tf = 512
    per_b = seq // tm
    row = pl.BlockSpec((tm, D_MODEL), lambda i, j: (i, 0))
    vec_b = pl.BlockSpec((1, 1, D_MODEL), lambda i, j: (i // per_b, 0, 0))
    vec = pl.BlockSpec((1, D_MODEL), lambda i, j: (0, 0))
    return pl.pallas_call(
        functools.partial(_ffn_kernel, final_norm=final_norm),
        grid=(t // tm, D_FF // tf),
        in_specs=[row, vec_b, vec_b, vec_b, vec, vec,
                  pl.BlockSpec((D_MODEL, tf), lambda i, j: (0, j)),
                  pl.BlockSpec((D_MODEL, tf), lambda i, j: (0, j)),
                  pl.BlockSpec((tf, D_MODEL), lambda i, j: (j, 0))],
        out_specs=row,
        out_shape=jax.ShapeDtypeStruct((t, D_MODEL), F32),
        scratch_shapes=[pltpu.VMEM((tm, D_MODEL), BF16)],
        compiler_params=_params("parallel", "arbitrary"),
        name="ffn",
    )(x2, shift, scale, gate, gain, fgain, wg, wu, wd)


def _inproj_kernel(x_ref, shift_ref, scale_ref, gain_ref, w_ref, o_ref, h_ref):
    @pl.when(pl.program_id(1) == 0)
    def _():
        for r in range(0, x_ref.shape[0], ROW_CHUNK):
            rows = slice(r, r + ROW_CHUNK)
            h = _modulated(x_ref[rows, :], gain_ref[...], scale_ref[0], shift_ref[0])
            h_ref[rows, :] = h.astype(BF16)

    res = _dot(h_ref[...], w_ref[...])
    for s in range(IN_STEP_BLOCKS):
        o_ref[s] = res[:, s * LANE:(s + 1) * LANE].astype(o_ref.dtype)


def _inproj(x2, seq, shift, scale, gain, w_in_p):
    t = x2.shape[0]
    tm = min(1024, seq)
    per_b = seq // tm
    tn = IN_STEP_BLOCKS * LANE
    return pl.pallas_call(
        _inproj_kernel,
        grid=(t // tm, N_BLK // IN_STEP_BLOCKS),
        in_specs=[
            pl.BlockSpec((tm, D_MODEL), lambda i, j: (i, 0)),
            pl.BlockSpec((1, 1, D_MODEL), lambda i, j: (i // per_b, 0, 0)),
            pl.BlockSpec((1, 1, D_MODEL), lambda i, j: (i // per_b, 0, 0)),
            pl.BlockSpec((1, D_MODEL), lambda i, j: (0, 0)),
            pl.BlockSpec((D_MODEL, tn), lambda i, j: (0, j)),
        ],
        out_specs=pl.BlockSpec((IN_STEP_BLOCKS, tm, LANE), lambda i, j: (j, i, 0)),
        out_shape=jax.ShapeDtypeStruct((N_BLK, t, LANE), BF16),
        scratch_shapes=[pltpu.VMEM((tm, D_MODEL), BF16)],
        compiler_params=_params("parallel", "arbitrary"),
        name="in_proj",
    )(x2, shift, scale, gain, w_in_p)


def _t5_bucket(rel):
    half = N_BUCKETS // 2
    max_exact = half // 2
    ret = jnp.where(rel > 0, half, 0)
    n = jnp.abs(rel)
    nf = jnp.maximum(n, 1).astype(F32)
    large = max_exact + (jnp.log(nf / max_exact) / math.log(MAX_DISTANCE / max_exact)
                         * (half - max_exact)).astype(jnp.int32)
    large = jnp.minimum(large, half - 1)
    return ret + jnp.where(n < max_exact, n, large)


def _far_bucket_is_constant(min_dist):
    half = N_BUCKETS // 2
    max_exact = half // 2
    val = math.log(min_dist / max_exact) / math.log(MAX_DISTANCE / max_exact) * (half - max_exact)
    return max_exact + val >= half


def _bias_tiles(rel_bias, tq):
    assert _far_bucket_is_constant(tq + 1)
    kpos = jnp.arange(tq)[:, None]
    qpos = jnp.arange(tq)[None, :]
    table = rel_bias.astype(F32) - rel_bias[N_BUCKETS // 2 - 1].astype(F32)

    def lookup(bucket):
        out = jnp.zeros((H_A, tq, tq), F32)
        for n in range(N_BUCKETS):
            out = jnp.where(bucket[None] == n, table[n][:, None, None], out)
        return out

    b0 = lookup(_t5_bucket(kpos - qpos))
    b1 = lookup(_t5_bucket(kpos - tq - qpos))
    visible = (kpos // CHUNK) <= (qpos // CHUNK)
    return jnp.where(visible[None], b0, MASK_VALUE), b1


ONES_ROWS = 16


def _qk_norm(x, gain):
    low = lax.broadcasted_iota(jnp.int32, (1, LANE), 1) < DH_A
    sq = x * x
    lo = jnp.sum(jnp.where(low, sq, 0.0), axis=-1, keepdims=True)
    hi = jnp.sum(jnp.where(low, 0.0, sq), axis=-1, keepdims=True)
    ms = jnp.where(low, lo, hi) * (1.0 / DH_A)
    return x * lax.rsqrt(ms + EPS) * gain


def _attn_kernel(lv_ref, og_ref, qg_ref, kg_ref, q_ref, k_ref, v_ref, b0_ref, b1_ref, o_ref,
                 kn_ref, vt_ref, qst_ref, sa_ref, sb_ref, m_ref, acc_ref, *, tq, lam_init):
    qi = pl.program_id(2)
    seq = k_ref.shape[2]

    @pl.when(qi == 0)
    def _():
        for blk in range(seq // tq):
            rows = slice(blk * tq, (blk + 1) * tq)
            kn_ref[rows, :] = _qk_norm(k_ref[0, 0, rows, :].astype(F32), kg_ref[...]).astype(kn_ref.dtype)
            vt_ref[0:LANE, rows] = v_ref[0, 0, rows, :].astype(F32).T.astype(vt_ref.dtype)
        vt_ref[LANE:, :] = jnp.ones((ONES_ROWS, seq), vt_ref.dtype)

    qt = _qk_norm(q_ref[0, 0].astype(F32), qg_ref[...]).T
    sub = lax.broadcasted_iota(jnp.int32, (LANE, 1), 0)
    qst_ref[...] = jnp.concatenate(
        [jnp.where(sub < DH_A, qt, 0.0), jnp.where(sub < DH_A, 0.0, qt)], axis=1).astype(qst_ref.dtype)

    m_ref[...] = jnp.full_like(m_ref, MASK_VALUE)
    acc_ref[...] = jnp.zeros_like(acc_ref)

    def logits(blk, bias, dst_ref):
        s = _dot(kn_ref[pl.ds(pl.multiple_of(blk * tq, tq), tq), :], qst_ref[...])
        if bias is not None:
            s = s + jnp.concatenate([bias, bias], axis=1)
        dst_ref[...] = s

    def accumulate(src_ref, blk):
        s = src_ref[...]
        vtb = vt_ref[:, pl.ds(pl.multiple_of(blk * tq, tq), tq)]
        m_prev = m_ref[0:1, :]
        m_new = jnp.maximum(m_prev, jnp.max(s, axis=0, keepdims=True))
        alpha = jnp.exp(m_prev - m_new)
        p = jnp.exp(s - m_new).astype(BF16)
        m_ref[...] = jnp.broadcast_to(m_new, m_ref.shape)
        acc_ref[...] = alpha * acc_ref[...] + _dot(vtb, p)

    n_far = jnp.maximum(qi - 1, 0)

    @pl.when(n_far > 0)
    def _():
        logits(0, None, sa_ref)

    def far_pair(i, carry):
        blk = 2 * i
        logits(blk + 1, None, sb_ref)
        accumulate(sa_ref, blk)
        logits(jnp.minimum(blk + 2, n_far - 1), None, sa_ref)
        accumulate(sb_ref, blk + 1)
        return carry

    lax.fori_loop(0, n_far // 2, far_pair, 0)

    @pl.when(n_far % 2 == 1)
    def _():
        accumulate(sa_ref, n_far - 1)

    @pl.when(qi >= 1)
    def _():
        logits(qi - 1, b1_ref[0], sa_ref)
        logits(qi, b0_ref[0], sb_ref)
        accumulate(sa_ref, qi - 1)
        accumulate(sb_ref, qi)

    @pl.when(qi == 0)
    def _():
        logits(0, b0_ref[0], sb_ref)
        accumulate(sb_ref, 0)

    acc = acc_ref[...]
    ot = acc[0:LANE] / acc[LANE:LANE + 1]
    lv = lv_ref[...]
    lam = (jnp.exp(jnp.sum(lv[0:1] * lv[1:2], axis=-1, keepdims=True))
           - jnp.exp(jnp.sum(lv[2:3] * lv[3:4], axis=-1, keepdims=True)) + lam_init)
    odt = ot[:, :tq] - lam * ot[:, tq:]
    yt = odt * lax.rsqrt(jnp.mean(odt * odt, axis=0, keepdims=True) + EPS)
    o_ref[...] = (yt.T * og_ref[...] * (1.0 - lam_init)).astype(o_ref.dtype)


def _attention(u4, lam_vec, out_gain, qk_gain, bias0, bias1, *, lam_init):
    _, bsz, seq, _ = u4.shape
    tq = bias0.shape[-1]
    nq = seq // tq
    kv_spec = lambda base: pl.BlockSpec((1, 1, seq, LANE), lambda b, h, i: (base + h, b, 0, 0))
    bias_spec = pl.BlockSpec((1, tq, tq), lambda b, h, i: (h, 0, 0))
    vec = pl.BlockSpec((1, LANE), lambda b, h, i: (0, 0))
    q_gain = jnp.tile(qk_gain[0] * (DH_A ** -0.5), 2).reshape(1, LANE)
    k_gain = jnp.tile(qk_gain[1], 2).reshape(1, LANE)
    return pl.pallas_call(
        functools.partial(_attn_kernel, tq=tq, lam_init=lam_init),
        grid=(bsz, H_A, nq),
        in_specs=[
            pl.BlockSpec((4, DH_A), lambda b, h, i: (0, 0)),
            vec, vec, vec,
            pl.BlockSpec((1, 1, tq, LANE), lambda b, h, i: (BLK_AQ + h, b, i, 0)),
            kv_spec(BLK_AK), kv_spec(BLK_AV), bias_spec, bias_spec,
        ],
        out_specs=pl.BlockSpec((tq, LANE), lambda b, h, i: (b * nq + i, h)),
        out_shape=jax.ShapeDtypeStruct((bsz * seq, MIX_WIDTH), BF16),
        scratch_shapes=[pltpu.VMEM((seq, LANE), BF16),
                        pltpu.VMEM((LANE + ONES_ROWS, seq), BF16),
                        pltpu.VMEM((LANE, 2 * tq), BF16),
                        pltpu.VMEM((tq, 2 * tq), F32),
                        pltpu.VMEM((tq, 2 * tq), F32),
                        pltpu.VMEM((8, 2 * tq), F32),
                        pltpu.VMEM((LANE + ONES_ROWS, 2 * tq), F32)],
        compiler_params=_params("parallel", "parallel", "arbitrary"),
        name="diff_attn",
    )(lam_vec, out_gain.reshape(1, LANE), q_gain, k_gain, u4, u4, u4, bias0, bias1)


GLA_TILE = 2 * CHUNK
HGRN_HEADS_PER_STEP = 4
GLA_HEADS_PER_STEP = 4


def _gla_block(q, k, v, g, st_ref):
    r_rows, dk = q.shape
    n8 = r_rows // SUBLANES
    n_chunks = r_rows // CHUNK
    per_chunk = CHUNK // SUBLANES
    sub = lax.broadcasted_iota(jnp.int32, (1, SUBLANES, dk), 1)
    q3, k3, g3 = (a.reshape(n8, SUBLANES, dk) for a in (q, k, g))

    p = g3
    sh = 1
    while sh < SUBLANES:
        p = p + jnp.where(sub >= sh, pltpu.roll(p, sh, axis=1), 0.0)
        sh *= 2
    p4 = p.reshape(n_chunks, per_chunk, SUBLANES, dk)
    total = p4[:, 0, SUBLANES - 1:, :]
    groups = [p4[:, 0]]
    for j in range(1, per_chunk):
        groups.append(p4[:, j] + total)
        total = total + p4[:, j, SUBLANES - 1:, :]
    b4 = jnp.stack(groups, axis=1)
    b3 = b4.reshape(n8, SUBLANES, dk)

    levels = []
    w = CHUNK // 2
    while w >= SUBLANES:
        m = w // SUBLANES
        pairs = (n8 // (2 * m), 2, m, SUBLANES, dk)
        b5, q5, k5 = b3.reshape(pairs), q3.reshape(pairs), k3.reshape(pairs)
        ref = b5[:, 0:1, m - 1:, SUBLANES - 1:, :]
        d = jnp.concatenate([ref - b5[:, 0:1], b5[:, 1:2] - ref], axis=1)
        x = jnp.concatenate([k5[:, 0:1], q5[:, 1:2]], axis=1) * jnp.exp(d)
        levels.append((w, x.reshape(r_rows, dk).astype(BF16)))
        w //= 2
    while w >= 1:
        odd = (sub & w) != 0
        if w == 4:
            ref = b3[:, 3:4, :]
            d = jnp.where(odd, b3 - ref, ref - b3)
        elif w == 2:
            ref = jnp.where(sub < 4, b3[:, 1:2, :], b3[:, 5:6, :])
            d = jnp.where(odd, b3 - ref, ref - b3)
        else:
            d = jnp.where(odd, g3, 0.0)
        x = jnp.where(odd, q3, k3) * jnp.exp(d)
        levels.append((w, x.reshape(r_rows, dk).astype(BF16)))
        w //= 2
    qb = q.astype(BF16)
    kb = k.astype(BF16)

    ti = lax.broadcasted_iota(jnp.int32, (GLA_TILE, 1), 0)
    si = lax.broadcasted_iota(jnp.int32, (1, GLA_TILE), 1)
    lev = jnp.where(si < ti, jnp.bitwise_xor(ti, si), 0)
    masks = [(lev >= w) & (lev < 2 * w) for w, _ in levels]
    diag = ti == si

    intra = []
    for sb in range(r_rows // GLA_TILE):
        sl = slice(sb * GLA_TILE, (sb + 1) * GLA_TILE)
        sc = jnp.where(diag, _dot_nt(qb[sl], kb[sl]), 0.0)
        for (w, x), mk in zip(levels, masks):
            sc = jnp.where(mk, _dot_nt(x[sl], x[sl]), sc)
        intra.append(_dot(sc.astype(BF16), v[sl]))

    chunked = (n_chunks, per_chunk, SUBLANES, dk)
    b_last = total[:, None]
    qe = (q3.reshape(chunked) * jnp.exp(b4)).reshape(r_rows, dk).astype(BF16)
    ke = (k3.reshape(chunked) * jnp.exp(b_last - b4)).reshape(r_rows, dk).astype(BF16)
    dec = jnp.exp(total)
    st = st_ref[...]
    inter = []
    for c in range(n_chunks):
        sl = slice(c * CHUNK, (c + 1) * CHUNK)
        inter.append(_dot_nt(qe[sl], st.astype(BF16)))
        st = st * dec[c] + _dot_tn(v[sl], ke[sl])
    st_ref[...] = st
    return jnp.concatenate(intra, axis=0) + jnp.concatenate(inter, axis=0)


def _hgrn_kernel(lb_ref, og_ref, q_ref, f_ref, i_ref, g_ref, o_ref, st_ref):
    @pl.when(pl.program_id(2) == 0)
    def _():
        st_ref[...] = jnp.zeros_like(st_ref)

    for hh in range(HGRN_HEADS_PER_STEP):
        lb = lb_ref[hh]
        zf = f_ref[hh, 0].astype(F32)
        t = jnp.exp(-jnp.abs(zf))
        inv = 1.0 / (1.0 + t)
        pos = zf >= 0.0
        sig = jnp.where(pos, 1.0, t) * inv
        log_f = jnp.where(lb > 0.0, jnp.log(lb + (1.0 - lb) * sig), jnp.minimum(zf, 0.0) - jnp.log(1.0 + t))
        k = (1.0 - lb) * (jnp.where(pos, t, 1.0) * inv)
        o = _gla_block(q_ref[hh, 0].astype(F32), k, i_ref[hh, 0], log_f, st_ref.at[hh])
        o = o * _sigmoid(g_ref[hh, 0].astype(F32))
        o_ref[:, hh * DV_B:(hh + 1) * DV_B] = (_rms(o) * og_ref[...]).astype(o_ref.dtype)


def _hgrn(u4, lb, out_gain, tc):
    _, bsz, seq, _ = u4.shape
    nt = seq // tc
    hp = HGRN_HEADS_PER_STEP
    blk = lambda base: pl.BlockSpec((hp, 1, tc, LANE), lambda b, h, i: (base // hp + h, b, i, 0))
    return pl.pallas_call(
        _hgrn_kernel,
        grid=(bsz, H_B // hp, nt),
        in_specs=[
            pl.BlockSpec((hp, 1, DK_B), lambda b, h, i: (h, 0, 0)),
            pl.BlockSpec((1, DV_B), lambda b, h, i: (0, 0)),
            blk(BLK_BQ), blk(BLK_BF), blk(BLK_BI), blk(BLK_BG),
        ],
        out_specs=pl.BlockSpec((tc, hp * DV_B), lambda b, h, i: (b * nt + i, h)),
        out_shape=jax.ShapeDtypeStruct((bsz * seq, MIX_WIDTH), BF16),
        scratch_shapes=[pltpu.VMEM((hp, DV_B, DK_B), F32)],
        compiler_params=_params("parallel", "parallel", "arbitrary"),
        name="hgrn2",
    )(lb.reshape(H_B, 1, DK_B), out_gain.reshape(1, DV_B), u4, u4, u4, u4)


def _glac_kernel(wup_ref, bup_ref, og_ref, q_ref, k_ref, v_ref, r_ref, gd_ref, o_ref, st_ref):
    @pl.when(pl.program_id(2) == 0)
    def _():
        st_ref[...] = jnp.zeros_like(st_ref)

    for hh in range(GLA_HEADS_PER_STEP):
        z = _dot(gd_ref[0, 0], wup_ref[:, hh * DK_C:(hh + 1) * DK_C]) + bup_ref[hh]
        log_a = _log_sigmoid(z) * (1.0 / GLA_GATE_NORM)
        q = q_ref[hh, 0].astype(F32) * (DK_C ** -0.5)
        v = jnp.concatenate([v_ref[2 * hh, 0], v_ref[2 * hh + 1, 0]], axis=1)
        o = _gla_block(q, k_ref[hh, 0].astype(F32), v, log_a, st_ref.at[hh])
        gate = jnp.concatenate([r_ref[2 * hh, 0], r_ref[2 * hh + 1, 0]], axis=1).astype(F32)
        o_ref[:, hh * DV_C:(hh + 1) * DV_C] = (
            _rms(o) * og_ref[...] * (gate * _sigmoid(gate))).astype(o_ref.dtype)


def _glac(u4, w_up_p, b_up, out_gain, tc):
    _, bsz, seq, _ = u4.shape
    nt = seq // tc
    hp = GLA_HEADS_PER_STEP
    one = lambda base: pl.BlockSpec((hp, 1, tc, LANE), lambda b, h, i: (base // hp + h, b, i, 0))
    two = lambda base: pl.BlockSpec((2 * hp, 1, tc, LANE), lambda b, h, i: (base // (2 * hp) + h, b, i, 0))
    return pl.pallas_call(
        _glac_kernel,
        grid=(bsz, H_C // hp, nt),
        in_specs=[
            pl.BlockSpec((LANE, hp * DK_C), lambda b, h, i: (0, h)),
            pl.BlockSpec((hp, 1, DK_C), lambda b, h, i: (h, 0, 0)),
            pl.BlockSpec((1, DV_C), lambda b, h, i: (0, 0)),
            one(BLK_CQ), one(BLK_CK), two(BLK_CV), two(BLK_CR),
            pl.BlockSpec((1, 1, tc, LANE), lambda b, h, i: (BLK_CGD, b, i, 0)),
        ],
        out_specs=pl.BlockSpec((tc, hp * DV_C), lambda b, h, i: (b * nt + i, h)),
        out_shape=jax.ShapeDtypeStruct((bsz * seq, MIX_WIDTH), BF16),
        scratch_shapes=[pltpu.VMEM((hp, DV_C, DK_C), F32)],
        compiler_params=_params("parallel", "parallel", "arbitrary"),
        name="gla",
    )(w_up_p, b_up.reshape(H_C, 1, DK_C), out_gain.reshape(1, DV_C), u4, u4, u4, u4, u4)


def _merge_kernel(ya_ref, yb_ref, yc_ref, gl_ref, wb_ref, wo_ref, x_ref, gate_ref, o_ref):
    n = pl.program_id(1)
    m_ref = o_ref

    def contribution(y_ref, branch):
        z = _dot(y_ref[...], wb_ref[branch])
        cols = []
        for c in range(D_MODEL // LANE):
            cols.append(_sigmoid(gl_ref[c].astype(F32)) * z[:, c * LANE:(c + 1) * LANE])
        return jnp.concatenate(cols, axis=1)

    @pl.when(n == 0)
    def _():
        m_ref[...] = contribution(ya_ref, 0)

    @pl.when(n == 1)
    def _():
        m_ref[...] += contribution(yb_ref, 1)

    @pl.when(n == 2)
    def _():
        m = m_ref[...] + contribution(yc_ref, 2)
        o_ref[...] = x_ref[...] + gate_ref[0] * _dot(m.astype(BF16), wo_ref[...])


def _merge(ya, yb, yc, u3, w_branch, w_out, x2, gate, seq):
    t = x2.shape[0]
    tm = min(512, seq)
    per_b = seq // tm
    ysp = pl.BlockSpec((tm, MIX_WIDTH), lambda i, n: (i, 0))
    row = pl.BlockSpec((tm, D_MODEL), lambda i, n: (i, 0))
    return pl.pallas_call(
        _merge_kernel,
        grid=(t // tm, N_BRANCH),
        in_specs=[
            ysp, ysp, ysp,
            pl.BlockSpec((D_MODEL // LANE, tm, LANE), lambda i, n: (n, i, 0)),
            pl.BlockSpec((N_BRANCH, MIX_WIDTH, D_MODEL), lambda i, n: (0, 0, 0), pipeline_mode=pl.Buffered(1)),
            pl.BlockSpec((D_MODEL, D_MODEL), lambda i, n: (0, 0), pipeline_mode=pl.Buffered(1)),
            row,
            pl.BlockSpec((1, 1, D_MODEL), lambda i, n: (i // per_b, 0, 0)),
        ],
        out_specs=row,
        out_shape=jax.ShapeDtypeStruct((t, D_MODEL), F32),
        compiler_params=_params("parallel", "arbitrary"),
        name="merge_out",
    )(ya, yb, yc, u3, w_branch, w_out, x2, gate)


def kernel(x, c, w_ada, b_ada, norm_gains, ffn_w_gate, ffn_w_up, ffn_w_down, w_in, qk_gains, diff_lambda,
           diff_out_gain, rel_bias, hgrn_lb_logits, hgrn_out_gain, gla_w_gate_up, gla_b_gate, gla_out_gain,
           w_branch, w_out):
    bsz, seq, _ = x.shape
    t = bsz * seq
    tq = min(512, seq)
    tc = min(512, seq)

    lb_all = jnp.cumsum(jax.nn.softmax(hgrn_lb_logits.astype(F32), axis=0), axis=0)
    lb_all = lb_all - lb_all[0]
    mod = _ada(c, w_ada, b_ada).reshape(DEPTH, bsz, 3, 3, 1, D_MODEL)
    bias0, bias1 = _bias_tiles(rel_bias, tq)

    x2 = x.reshape(t, D_MODEL)
    for l in range(DEPTH):
        shift, scale, gate = mod[l, :, :, 0], mod[l, :, :, 1], mod[l, :, :, 2]
        gains = norm_gains[l].reshape(4, 1, D_MODEL)
        lam_init = 0.8 - 0.6 * math.exp(-0.3 * l)

        x2 = _ffn(x2, seq, shift[:, 0], scale[:, 0], gate[:, 0], gains[0], gains[3],
                  ffn_w_gate[l, 0].astype(BF16), ffn_w_up[l, 0].astype(BF16), ffn_w_down[l, 0].astype(BF16),
                  final_norm=False)

        w_l = w_in[l]
        n_real = N_MAIN + GLA_GATE_RANK
        w_in_p = jnp.concatenate(
            [w_l[:, n_real:].astype(BF16), w_l[:, :n_real].astype(BF16),
             jnp.zeros((D_MODEL, N_IN_PAD - w_l.shape[1]), BF16)], axis=1)
        u3 = _inproj(x2, seq, shift[:, 1], scale[:, 1], gains[1], w_in_p)
        u4 = u3.reshape(N_BLK, bsz, seq, LANE)

        ya = _attention(u4, diff_lambda[l], diff_out_gain[l], qk_gains[l], bias0, bias1, lam_init=lam_init)
        yb = _hgrn(u4, lb_all[l], hgrn_out_gain[l], tc)
        w_up_p = jnp.concatenate(
            [gla_w_gate_up[l], jnp.zeros((LANE - GLA_GATE_RANK, H_C * DK_C), F32)], axis=0).astype(BF16)
        yc = _glac(u4, w_up_p, gla_b_gate[l], gla_out_gain[l], tc)

        x2 = _merge(ya, yb, yc, u3, w_branch[l].astype(BF16), w_out[l].astype(BF16), x2, gate[:, 1], seq)

        x2 = _ffn(x2, seq, shift[:, 2], scale[:, 2], gate[:, 2], gains[2], gains[3],
                  ffn_w_gate[l, 1].astype(BF16), ffn_w_up[l, 1].astype(BF16), ffn_w_down[l, 1].astype(BF16),
                  final_norm=True)
    return x2.reshape(bsz, seq, D_MODEL)
```

```python
import functools
import math

import jax
import jax.numpy as jnp
from jax import lax
from jax.experimental import pallas as pl
from jax.experimental.pallas import tpu as pltpu

D_MODEL = 2048
DEPTH = 2
CHUNK = 64
MIX_WIDTH = D_MODEL // 2
N_BRANCH = 3
H_A = 8
DH_A = 64
H_B = 8
DK_B = 128
DV_B = 128
H_C = 4
DK_C = 128
DV_C = 256
GLA_GATE_RANK = 16
GLA_GATE_NORM = 16.0
D_FF = 11 * D_MODEL // 4
N_BUCKETS = 32
MAX_DISTANCE = 128
EPS = 1e-6

LANE = 128
SUBLANES = 8
ROW_CHUNK = 256
MASK_VALUE = -1e30
VMEM_LIMIT = 56 * 1024 * 1024
N_GATE_BLK = N_BRANCH * D_MODEL // LANE
BLK_AQ = N_GATE_BLK
BLK_AK = BLK_AQ + H_A
BLK_AV = BLK_AK + H_A
BLK_BQ = BLK_AV + H_A
BLK_BF = BLK_BQ + H_B
BLK_BI = BLK_BF + H_B
BLK_BG = BLK_BI + H_B
BLK_CQ = BLK_BG + H_B
BLK_CK = BLK_CQ + H_C
BLK_CV = BLK_CK + H_C
BLK_CR = BLK_CV + 2 * H_C
BLK_CGD = BLK_CR + 2 * H_C
N_MAIN = 10 * MIX_WIDTH
IN_STEP_BLOCKS = 12
N_BLK = 132
N_IN_PAD = N_BLK * LANE

BF16 = jnp.bfloat16
F32 = jnp.float32


def _params(*sem):
    return pltpu.CompilerParams(dimension_semantics=sem, vmem_limit_bytes=VMEM_LIMIT)


def _sigmoid(x):
    return 1.0 / (1.0 + jnp.exp(-x))


def _log_sigmoid(x):
    return jnp.minimum(x, 0.0) - jnp.log(1.0 + jnp.exp(-jnp.abs(x)))


def _rms(x):
    return x * lax.rsqrt(jnp.mean(x * x, axis=-1, keepdims=True) + EPS)


def _dot(a, b):
    return jnp.dot(a, b, preferred_element_type=F32)


def _dot_nt(a, b):
    return lax.dot_general(a, b, (((1,), (1,)), ((), ())), preferred_element_type=F32)


def _dot_tn(a, b):
    return lax.dot_general(a, b, (((0,), (0,)), ((), ())), preferred_element_type=F32)


def _ada_kernel(c_ref, w_ref, b_ref, o_ref):
    c = c_ref[...]
    cond = c * _sigmoid(c)
    c_hi = cond.astype(BF16)
    c_lo = (cond - c_hi.astype(F32)).astype(BF16)
    w = w_ref[0]
    w_hi = w.astype(BF16)
    w_lo = (w - w_hi.astype(F32)).astype(BF16)
    o_ref[0] = _dot(c_hi, w_hi) + (_dot(c_lo, w_hi) + _dot(c_hi, w_lo)) + b_ref[0]


def _ada(c, w_ada, b_ada):
    bsz = c.shape[0]
    n = w_ada.shape[-1]
    tn = 1024
    return pl.pallas_call(
        _ada_kernel,
        grid=(DEPTH, n // tn),
        in_specs=[
            pl.BlockSpec((bsz, D_MODEL), lambda l, j: (0, 0)),
            pl.BlockSpec((1, D_MODEL, tn), lambda l, j: (l, 0, j)),
            pl.BlockSpec((1, 1, tn), lambda l, j: (l, 0, j)),
        ],
        out_specs=pl.BlockSpec((1, bsz, tn), lambda l, j: (l, 0, j)),
        out_shape=jax.ShapeDtypeStruct((DEPTH, bsz, n), F32),
        compiler_params=_params("parallel", "parallel"),
        name="ada_mod",
    )(c, w_ada, b_ada.reshape(DEPTH, 1, n))


def _modulated(x, gain, scale, shift):
    return (_rms(x) * gain) * (1.0 + scale) + shift


def _ffn_kernel(x_ref, shift_ref, scale_ref, gate_ref, gain_ref, fgain_ref, wg_ref, wu_ref, wd_ref,
                o_ref, h_ref, a_ref, *, final_norm):
    j = pl.program_id(1)
    last = pl.num_programs(1) - 1
    row_chunks = [slice(r, r + ROW_CHUNK) for r in range(0, x_ref.shape[0], ROW_CHUNK)]

    def gated():
        h = h_ref[...]
        g = _dot(h, wg_ref[...])
        u = _dot(h, wu_ref[...])
        return (g * _sigmoid(g) * u).astype(BF16)

    @pl.when(j == 0)
    def _():
        for rows in row_chunks:
            h = _modulated(x_ref[rows, :], gain_ref[...], scale_ref[0], shift_ref[0])
            h_ref[rows, :] = h.astype(BF16)
        o_ref[...] = jnp.zeros_like(o_ref)
        a_ref[...] = gated()

    @pl.when(jnp.logical_and(j > 0, j < last))
    def _():
        down = _dot(a_ref[...], wd_ref[...])
        a_next = gated()
        o_ref[...] += down
        a_ref[...] = a_next

    @pl.when(j == last)
    def _():
        o_ref[...] += _dot(a_ref[...], wd_ref[...])
        for rows in row_chunks:
            y = x_ref[rows, :] + 0.5 * gate_ref[0] * o_ref[rows, :]
            if final_norm:
                y = _rms(y) * fgain_ref[...]
            o_ref[rows, :] = y


def _ffn(x2, seq, shift, scale, gate, gain, fgain, wg, wu, wd, *, final_norm):
    t = x2.shape[0]
    tm = min(512, seq)
    tf = 512
    nf = D_FF // tf
    per_b = seq // tm
    row = pl.BlockSpec((tm, D_MODEL), lambda i, j: (i, 0))
    vec_b = pl.BlockSpec((1, 1, D_MODEL), lambda i, j: (i // per_b, 0, 0))
    vec = pl.BlockSpec((1, D_MODEL), lambda i, j: (0, 0))
    return pl.pallas_call(
        functools.partial(_ffn_kernel, final_norm=final_norm),
        grid=(t // tm, nf + 1),
        in_specs=[row, vec_b, vec_b, vec_b, vec, vec,
                  pl.BlockSpec((D_MODEL, tf), lambda i, j: (0, jnp.minimum(j, nf - 1))),
                  pl.BlockSpec((D_MODEL, tf), lambda i, j: (0, jnp.minimum(j, nf - 1))),
                  pl.BlockSpec((tf, D_MODEL), lambda i, j: (jnp.maximum(j - 1, 0), 0))],
        out_specs=row,
        out_shape=jax.ShapeDtypeStruct((t, D_MODEL), F32),
        scratch_shapes=[pltpu.VMEM((tm, D_MODEL), BF16), pltpu.VMEM((tm, tf), BF16)],
        compiler_params=_params("parallel", "arbitrary"),
        name="ffn",
    )(x2, shift, scale, gate, gain, fgain, wg, wu, wd)


def _inproj_kernel(x_ref, shift_ref, scale_ref, gain_ref, w_ref, o_ref, h_ref):
    @pl.when(pl.program_id(1) == 0)
    def _():
        for r in range(0, x_ref.shape[0], ROW_CHUNK):
            rows = slice(r, r + ROW_CHUNK)
            h = _modulated(x_ref[rows, :], gain_ref[...], scale_ref[0], shift_ref[0])
            h_ref[rows, :] = h.astype(BF16)

    res = _dot(h_ref[...], w_ref[...])
    for s in range(IN_STEP_BLOCKS):
        o_ref[s] = res[:, s * LANE:(s + 1) * LANE].astype(o_ref.dtype)


def _inproj(x2, seq, shift, scale, gain, w_in_p):
    t = x2.shape[0]
    tm = min(1024, seq)
    per_b = seq // tm
    tn = IN_STEP_BLOCKS * LANE
    return pl.pallas_call(
        _inproj_kernel,
        grid=(t // tm, N_BLK // IN_STEP_BLOCKS),
        in_specs=[
            pl.BlockSpec((tm, D_MODEL), lambda i, j: (i, 0)),
            pl.BlockSpec((1, 1, D_MODEL), lambda i, j: (i // per_b, 0, 0)),
            pl.BlockSpec((1, 1, D_MODEL), lambda i, j: (i // per_b, 0, 0)),
            pl.BlockSpec((1, D_MODEL), lambda i, j: (0, 0)),
            pl.BlockSpec((D_MODEL, tn), lambda i, j: (0, j)),
        ],
        out_specs=pl.BlockSpec((IN_STEP_BLOCKS, tm, LANE), lambda i, j: (j, i, 0)),
        out_shape=jax.ShapeDtypeStruct((N_BLK, t, LANE), BF16),
        scratch_shapes=[pltpu.VMEM((tm, D_MODEL), BF16)],
        compiler_params=_params("parallel", "arbitrary"),
        name="in_proj",
    )(x2, shift, scale, gain, w_in_p)


def _t5_bucket(rel):
    half = N_BUCKETS // 2
    max_exact = half // 2
    ret = jnp.where(rel > 0, half, 0)
    n = jnp.abs(rel)
    nf = jnp.maximum(n, 1).astype(F32)
    large = max_exact + (jnp.log(nf / max_exact) / math.log(MAX_DISTANCE / max_exact)
                         * (half - max_exact)).astype(jnp.int32)
    large = jnp.minimum(large, half - 1)
    return ret + jnp.where(n < max_exact, n, large)


def _far_bucket_is_constant(min_dist):
    half = N_BUCKETS // 2
    max_exact = half // 2
    val = math.log(min_dist / max_exact) / math.log(MAX_DISTANCE / max_exact) * (half - max_exact)
    return max_exact + val >= half


def _bias_tiles(rel_bias, tq):
    assert _far_bucket_is_constant(tq + 1)
    kpos = jnp.arange(tq)[:, None]
    qpos = jnp.arange(tq)[None, :]
    table = rel_bias.astype(F32) - rel_bias[N_BUCKETS // 2 - 1].astype(F32)

    def lookup(bucket):
        out = jnp.zeros((H_A, tq, tq), F32)
        for n in range(N_BUCKETS):
            out = jnp.where(bucket[None] == n, table[n][:, None, None], out)
        return out

    b0 = lookup(_t5_bucket(kpos - qpos))
    b1 = lookup(_t5_bucket(kpos - tq - qpos))
    visible = (kpos // CHUNK) <= (qpos // CHUNK)
    return jnp.where(visible[None], b0, MASK_VALUE), b1


ONES_ROWS = 16
ATTN_HEADS_PER_STEP = 2


def _qk_norm(x, gain):
    low = lax.broadcasted_iota(jnp.int32, (1, LANE), 1) < DH_A
    sq = x * x
    lo = jnp.sum(jnp.where(low, sq, 0.0), axis=-1, keepdims=True)
    hi = jnp.sum(jnp.where(low, 0.0, sq), axis=-1, keepdims=True)
    ms = jnp.where(low, lo, hi) * (1.0 / DH_A)
    return x * lax.rsqrt(ms + EPS) * gain


def _attn_kernel(lv_ref, og_ref, qg_ref, kg_ref, q_ref, k_ref, v_ref, b0_ref, b1_ref, o_ref,
                 kn_ref, vt_ref, qst_ref, sa_ref, sb_ref, m_ref, acc_ref, *, tq, lam_init):
    qi = pl.program_id(2)
    seq = k_ref.shape[2]

    heads = range(ATTN_HEADS_PER_STEP)

    @pl.when(qi == 0)
    def _():
        for hh in heads:
            for blk in range(seq // tq):
                rows = slice(blk * tq, (blk + 1) * tq)
                kn_ref[hh, rows, :] = _qk_norm(
                    k_ref[hh, 0, rows, :].astype(F32), kg_ref[...]).astype(kn_ref.dtype)
                vt_ref[hh, 0:LANE, rows] = v_ref[hh, 0, rows, :].astype(F32).T.astype(vt_ref.dtype)
            vt_ref[hh, LANE:, :] = jnp.ones((ONES_ROWS, seq), vt_ref.dtype)

    sub = lax.broadcasted_iota(jnp.int32, (LANE, 1), 0)
    for hh in heads:
        qt = _qk_norm(q_ref[hh, 0].astype(F32), qg_ref[...]).T
        qst_ref[hh] = jnp.concatenate(
            [jnp.where(sub < DH_A, qt, 0.0), jnp.where(sub < DH_A, 0.0, qt)], axis=1).astype(qst_ref.dtype)

    m_ref[...] = jnp.full_like(m_ref, MASK_VALUE)
    acc_ref[...] = jnp.zeros_like(acc_ref)

    def logits(blk, bias_ref, dst_ref):
        for hh in heads:
            s = _dot(kn_ref[hh, pl.ds(pl.multiple_of(blk * tq, tq), tq), :], qst_ref[hh])
            if bias_ref is not None:
                bias = bias_ref[hh]
                s = s + jnp.concatenate([bias, bias], axis=1)
            dst_ref[hh] = s

    def accumulate(src_ref, blk):
        for hh in heads:
            s = src_ref[hh]
            vtb = vt_ref[hh, :, pl.ds(pl.multiple_of(blk * tq, tq), tq)]
            m_prev = m_ref[hh, 0:1, :]
            m_new = jnp.maximum(m_prev, jnp.max(s, axis=0, keepdims=True))
            alpha = jnp.exp(m_prev - m_new)
            p = jnp.exp(s - m_new).astype(BF16)
            m_ref[hh] = jnp.broadcast_to(m_new, m_ref.shape[1:])
            acc_ref[hh] = alpha * acc_ref[hh] + _dot(vtb, p)

    n_far = jnp.maximum(qi - 1, 0)

    @pl.when(n_far > 0)
    def _():
        logits(0, None, sa_ref)

    def far_pair(i, carry):
        blk = 2 * i
        logits(blk + 1, None, sb_ref)
        accumulate(sa_ref, blk)
        logits(jnp.minimum(blk + 2, n_far - 1), None, sa_ref)
        accumulate(sb_ref, blk + 1)
        return carry

    lax.fori_loop(0, n_far // 2, far_pair, 0)

    @pl.when(n_far % 2 == 1)
    def _():
        accumulate(sa_ref, n_far - 1)

    @pl.when(qi >= 1)
    def _():
        logits(qi - 1, b1_ref, sa_ref)
        logits(qi, b0_ref, sb_ref)
        accumulate(sa_ref, qi - 1)
        accumulate(sb_ref, qi)

    @pl.when(qi == 0)
    def _():
        logits(0, b0_ref, sb_ref)
        accumulate(sb_ref, 0)

    lv = lv_ref[...]
    lam = (jnp.exp(jnp.sum(lv[0:1] * lv[1:2], axis=-1, keepdims=True))
           - jnp.exp(jnp.sum(lv[2:3] * lv[3:4], axis=-1, keepdims=True)) + lam_init)
    for hh in heads:
        acc = acc_ref[hh]
        ot = acc[0:LANE] / acc[LANE:LANE + 1]
        odt = ot[:, :tq] - lam * ot[:, tq:]
        yt = odt * lax.rsqrt(jnp.mean(odt * odt, axis=0, keepdims=True) + EPS)
        o_ref[:, hh * LANE:(hh + 1) * LANE] = (yt.T * og_ref[...] * (1.0 - lam_init)).astype(o_ref.dtype)


def _attention(u4, lam_vec, out_gain, qk_gain, bias0, bias1, *, lam_init):
    _, bsz, seq, _ = u4.shape
    tq = bias0.shape[-1]
    nq = seq // tq
    hp = ATTN_HEADS_PER_STEP
    kv_spec = lambda base: pl.BlockSpec((hp, 1, seq, LANE), lambda b, h, i: (base // hp + h, b, 0, 0))
    bias_spec = pl.BlockSpec((hp, tq, tq), lambda b, h, i: (h, 0, 0))
    vec = pl.BlockSpec((1, LANE), lambda b, h, i: (0, 0))
    q_gain = jnp.tile(qk_gain[0] * (DH_A ** -0.5), 2).reshape(1, LANE)
    k_gain = jnp.tile(qk_gain[1], 2).reshape(1, LANE)
    return pl.pallas_call(
        functools.partial(_attn_kernel, tq=tq, lam_init=lam_init),
        grid=(bsz, H_A // hp, nq),
        in_specs=[
            pl.BlockSpec((4, DH_A), lambda b, h, i: (0, 0)),
            vec, vec, vec,
            pl.BlockSpec((hp, 1, tq, LANE), lambda b, h, i: (BLK_AQ // hp + h, b, i, 0)),
            kv_spec(BLK_AK), kv_spec(BLK_AV), bias_spec, bias_spec,
        ],
        out_specs=pl.BlockSpec((tq, hp * LANE), lambda b, h, i: (b * nq + i, h)),
        out_shape=jax.ShapeDtypeStruct((bsz * seq, MIX_WIDTH), BF16),
        scratch_shapes=[pltpu.VMEM((hp, seq, LANE), BF16),
                        pltpu.VMEM((hp, LANE + ONES_ROWS, seq), BF16),
                        pltpu.VMEM((hp, LANE, 2 * tq), BF16),
                        pltpu.VMEM((hp, tq, 2 * tq), F32),
                        pltpu.VMEM((hp, tq, 2 * tq), F32),
                        pltpu.VMEM((hp, 8, 2 * tq), F32),
                        pltpu.VMEM((hp, LANE + ONES_ROWS, 2 * tq), F32)],
        compiler_params=_params("parallel", "parallel", "arbitrary"),
        name="diff_attn",
    )(lam_vec, out_gain.reshape(1, LANE), q_gain, k_gain, u4, u4, u4, bias0, bias1)


GLA_TILE = 2 * CHUNK
HGRN_HEADS_PER_STEP = 4
GLA_HEADS_PER_STEP = 4


def _gla_block(q, k, v, g, st_ref):
    r_rows, dk = q.shape
    n8 = r_rows // SUBLANES
    n_chunks = r_rows // CHUNK
    per_chunk = CHUNK // SUBLANES
    sub = lax.broadcasted_iota(jnp.int32, (1, SUBLANES, dk), 1)
    q3, k3, g3 = (a.reshape(n8, SUBLANES, dk) for a in (q, k, g))

    p = g3
    sh = 1
    while sh < SUBLANES:
        p = p + jnp.where(sub >= sh, pltpu.roll(p, sh, axis=1), 0.0)
        sh *= 2
    p4 = p.reshape(n_chunks, per_chunk, SUBLANES, dk)
    total = p4[:, 0, SUBLANES - 1:, :]
    groups = [p4[:, 0]]
    for j in range(1, per_chunk):
        groups.append(p4[:, j] + total)
        total = total + p4[:, j, SUBLANES - 1:, :]
    b4 = jnp.stack(groups, axis=1)
    b3 = b4.reshape(n8, SUBLANES, dk)

    levels = []
    w = CHUNK // 2
    while w >= SUBLANES:
        m = w // SUBLANES
        pairs = (n8 // (2 * m), 2, m, SUBLANES, dk)
        b5, q5, k5 = b3.reshape(pairs), q3.reshape(pairs), k3.reshape(pairs)
        ref = b5[:, 0:1, m - 1:, SUBLANES - 1:, :]
        d = jnp.concatenate([ref - b5[:, 0:1], b5[:, 1:2] - ref], axis=1)
        x = jnp.concatenate([k5[:, 0:1], q5[:, 1:2]], axis=1) * jnp.exp(d)
        levels.append((w, x.reshape(r_rows, dk).astype(BF16)))
        w //= 2
    while w >= 1:
        odd = (sub & w) != 0
        if w == 4:
            ref = b3[:, 3:4, :]
            d = jnp.where(odd, b3 - ref, ref - b3)
        elif w == 2:
            ref = jnp.where(sub < 4, b3[:, 1:2, :], b3[:, 5:6, :])
            d = jnp.where(odd, b3 - ref, ref - b3)
        else:
            d = jnp.where(odd, g3, 0.0)
        x = jnp.where(odd, q3, k3) * jnp.exp(d)
        levels.append((w, x.reshape(r_rows, dk).astype(BF16)))
        w //= 2
    qb = q.astype(BF16)
    kb = k.astype(BF16)

    ti = lax.broadcasted_iota(jnp.int32, (GLA_TILE, 1), 0)
    si = lax.broadcasted_iota(jnp.int32, (1, GLA_TILE), 1)
    lev = jnp.where(si < ti, jnp.bitwise_xor(ti, si), 0)
    masks = [(lev >= w) & (lev < 2 * w) for w, _ in levels]
    diag = ti == si

    intra = []
    for sb in range(r_rows // GLA_TILE):
        sl = slice(sb * GLA_TILE, (sb + 1) * GLA_TILE)
        sc = jnp.where(diag, _dot_nt(qb[sl], kb[sl]), 0.0)
        for (w, x), mk in zip(levels, masks):
            sc = jnp.where(mk, _dot_nt(x[sl], x[sl]), sc)
        intra.append(_dot(sc.astype(BF16), v[sl]))

    chunked = (n_chunks, per_chunk, SUBLANES, dk)
    b_last = total[:, None]
    qe = (q3.reshape(chunked) * jnp.exp(b4)).reshape(r_rows, dk).astype(BF16)
    ke = (k3.reshape(chunked) * jnp.exp(b_last - b4)).reshape(r_rows, dk).astype(BF16)
    dec = jnp.exp(total)
    st = st_ref[...]
    inter = []
    for c in range(n_chunks):
        sl = slice(c * CHUNK, (c + 1) * CHUNK)
        inter.append(_dot_nt(qe[sl], st.astype(BF16)))
        st = st * dec[c] + _dot_tn(v[sl], ke[sl])
    st_ref[...] = st
    return jnp.concatenate(intra, axis=0) + jnp.concatenate(inter, axis=0)


def _hgrn_kernel(lb_ref, og_ref, q_ref, f_ref, i_ref, g_ref, o_ref, st_ref):
    @pl.when(pl.program_id(2) == 0)
    def _():
        st_ref[...] = jnp.zeros_like(st_ref)

    for hh in range(HGRN_HEADS_PER_STEP):
        lb = lb_ref[hh]
        zf = f_ref[hh, 0].astype(F32)
        t = jnp.exp(-jnp.abs(zf))
        inv = 1.0 / (1.0 + t)
        pos = zf >= 0.0
        sig = jnp.where(pos, 1.0, t) * inv
        log_f = jnp.where(lb > 0.0, jnp.log(lb + (1.0 - lb) * sig), jnp.minimum(zf, 0.0) - jnp.log(1.0 + t))
        k = (1.0 - lb) * (jnp.where(pos, t, 1.0) * inv)
        o = _gla_block(q_ref[hh, 0].astype(F32), k, i_ref[hh, 0], log_f, st_ref.at[hh])
        o = o * _sigmoid(g_ref[hh, 0].astype(F32))
        o_ref[:, hh * DV_B:(hh + 1) * DV_B] = (_rms(o) * og_ref[...]).astype(o_ref.dtype)


def _hgrn(u4, lb, out_gain, tc):
    _, bsz, seq, _ = u4.shape
    nt = seq // tc
    hp = HGRN_HEADS_PER_STEP
    blk = lambda base: pl.BlockSpec((hp, 1, tc, LANE), lambda b, h, i: (base // hp + h, b, i, 0))
    return pl.pallas_call(
        _hgrn_kernel,
        grid=(bsz, H_B // hp, nt),
        in_specs=[
            pl.BlockSpec((hp, 1, DK_B), lambda b, h, i: (h, 0, 0)),
            pl.BlockSpec((1, DV_B), lambda b, h, i: (0, 0)),
            blk(BLK_BQ), blk(BLK_BF), blk(BLK_BI), blk(BLK_BG),
        ],
        out_specs=pl.BlockSpec((tc, hp * DV_B), lambda b, h, i: (b * nt + i, h)),
        out_shape=jax.ShapeDtypeStruct((bsz * seq, MIX_WIDTH), BF16),
        scratch_shapes=[pltpu.VMEM((hp, DV_B, DK_B), F32)],
        compiler_params=_params("parallel", "parallel", "arbitrary"),
        name="hgrn2",
    )(lb.reshape(H_B, 1, DK_B), out_gain.reshape(1, DV_B), u4, u4, u4, u4)


def _glac_kernel(wup_ref, bup_ref, og_ref, q_ref, k_ref, v_ref, r_ref, gd_ref, o_ref, st_ref):
    @pl.when(pl.program_id(2) == 0)
    def _():
        st_ref[...] = jnp.zeros_like(st_ref)

    for hh in range(GLA_HEADS_PER_STEP):
        z = _dot(gd_ref[0, 0], wup_ref[:, hh * DK_C:(hh + 1) * DK_C]) + bup_ref[hh]
        log_a = _log_sigmoid(z) * (1.0 / GLA_GATE_NORM)
        q = q_ref[hh, 0].astype(F32) * (DK_C ** -0.5)
        v = jnp.concatenate([v_ref[2 * hh, 0], v_ref[2 * hh + 1, 0]], axis=1)
        o = _gla_block(q, k_ref[hh, 0].astype(F32), v, log_a, st_ref.at[hh])
        gate = jnp.concatenate([r_ref[2 * hh, 0], r_ref[2 * hh + 1, 0]], axis=1).astype(F32)
        o_ref[:, hh * DV_C:(hh + 1) * DV_C] = (
            _rms(o) * og_ref[...] * (gate * _sigmoid(gate))).astype(o_ref.dtype)


def _glac(u4, w_up_p, b_up, out_gain, tc):
    _, bsz, seq, _ = u4.shape
    nt = seq // tc
    hp = GLA_HEADS_PER_STEP
    one = lambda base: pl.BlockSpec((hp, 1, tc, LANE), lambda b, h, i: (base // hp + h, b, i, 0))
    two = lambda base: pl.BlockSpec((2 * hp, 1, tc, LANE), lambda b, h, i: (base // (2 * hp) + h, b, i, 0))
    return pl.pallas_call(
        _glac_kernel,
        grid=(bsz, H_C // hp, nt),
        in_specs=[
            pl.BlockSpec((LANE, hp * DK_C), lambda b, h, i: (0, h)),
            pl.BlockSpec((hp, 1, DK_C), lambda b, h, i: (h, 0, 0)),
            pl.BlockSpec((1, DV_C), lambda b, h, i: (0, 0)),
            one(BLK_CQ), one(BLK_CK), two(BLK_CV), two(BLK_CR),
            pl.BlockSpec((1, 1, tc, LANE), lambda b, h, i: (BLK_CGD, b, i, 0)),
        ],
        out_specs=pl.BlockSpec((tc, hp * DV_C), lambda b, h, i: (b * nt + i, h)),
        out_shape=jax.ShapeDtypeStruct((bsz * seq, MIX_WIDTH), BF16),
        scratch_shapes=[pltpu.VMEM((hp, DV_C, DK_C), F32)],
        compiler_params=_params("parallel", "parallel", "arbitrary"),
        name="gla",
    )(w_up_p, b_up.reshape(H_C, 1, DK_C), out_gain.reshape(1, DV_C), u4, u4, u4, u4, u4)


def _merge_kernel(ya_ref, yb_ref, yc_ref, gl_ref, wb_ref, wo_ref, x_ref, gate_ref, o_ref):
    n = pl.program_id(1)
    m_ref = o_ref

    def contribution(y_ref, branch):
        z = _dot(y_ref[...], wb_ref[branch])
        cols = []
        for c in range(D_MODEL // LANE):
            cols.append(_sigmoid(gl_ref[c].astype(F32)) * z[:, c * LANE:(c + 1) * LANE])
        return jnp.concatenate(cols, axis=1)

    @pl.when(n == 0)
    def _():
        m_ref[...] = contribution(ya_ref, 0)

    @pl.when(n == 1)
    def _():
        m_ref[...] += contribution(yb_ref, 1)

    @pl.when(n == 2)
    def _():
        m = m_ref[...] + contribution(yc_ref, 2)
        o_ref[...] = x_ref[...] + gate_ref[0] * _dot(m.astype(BF16), wo_ref[...])


def _merge(ya, yb, yc, u3, w_branch, w_out, x2, gate, seq):
    t = x2.shape[0]
    tm = min(512, seq)
    per_b = seq // tm
    ysp = pl.BlockSpec((tm, MIX_WIDTH), lambda i, n: (i, 0))
    row = pl.BlockSpec((tm, D_MODEL), lambda i, n: (i, 0))
    return pl.pallas_call(
        _merge_kernel,
        grid=(t // tm, N_BRANCH),
        in_specs=[
            ysp, ysp, ysp,
            pl.BlockSpec((D_MODEL // LANE, tm, LANE), lambda i, n: (n, i, 0)),
            pl.BlockSpec((N_BRANCH, MIX_WIDTH, D_MODEL), lambda i, n: (0, 0, 0), pipeline_mode=pl.Buffered(1)),
            pl.BlockSpec((D_MODEL, D_MODEL), lambda i, n: (0, 0), pipeline_mode=pl.Buffered(1)),
            row,
            pl.BlockSpec((1, 1, D_MODEL), lambda i, n: (i // per_b, 0, 0)),
        ],
        out_specs=row,
        out_shape=jax.ShapeDtypeStruct((t, D_MODEL), F32),
        compiler_params=_params("parallel", "arbitrary"),
        name="merge_out",
    )(ya, yb, yc, u3, w_branch, w_out, x2, gate)


def kernel(x, c, w_ada, b_ada, norm_gains, ffn_w_gate, ffn_w_up, ffn_w_down, w_in, qk_gains, diff_lambda,
           diff_out_gain, rel_bias, hgrn_lb_logits, hgrn_out_gain, gla_w_gate_up, gla_b_gate, gla_out_gain,
           w_branch, w_out):
    bsz, seq, _ = x.shape
    t = bsz * seq
    tq = min(512, seq)
    tc = min(512, seq)

    lb_all = jnp.cumsum(jax.nn.softmax(hgrn_lb_logits.astype(F32), axis=0), axis=0)
    lb_all = lb_all - lb_all[0]
    mod = _ada(c, w_ada, b_ada).reshape(DEPTH, bsz, 3, 3, 1, D_MODEL)
    bias0, bias1 = _bias_tiles(rel_bias, tq)

    x2 = x.reshape(t, D_MODEL)
    for l in range(DEPTH):
        shift, scale, gate = mod[l, :, :, 0], mod[l, :, :, 1], mod[l, :, :, 2]
        gains = norm_gains[l].reshape(4, 1, D_MODEL)
        lam_init = 0.8 - 0.6 * math.exp(-0.3 * l)

        x2 = _ffn(x2, seq, shift[:, 0], scale[:, 0], gate[:, 0], gains[0], gains[3],
                  ffn_w_gate[l, 0].astype(BF16), ffn_w_up[l, 0].astype(BF16), ffn_w_down[l, 0].astype(BF16),
                  final_norm=False)

        w_l = w_in[l]
        n_real = N_MAIN + GLA_GATE_RANK
        w_in_p = jnp.concatenate(
            [w_l[:, n_real:].astype(BF16), w_l[:, :n_real].astype(BF16),
             jnp.zeros((D_MODEL, N_IN_PAD - w_l.shape[1]), BF16)], axis=1)
        u3 = _inproj(x2, seq, shift[:, 1], scale[:, 1], gains[1], w_in_p)
        u4 = u3.reshape(N_BLK, bsz, seq, LANE)

        ya = _attention(u4, diff_lambda[l], diff_out_gain[l], qk_gains[l], bias0, bias1, lam_init=lam_init)
        yb = _hgrn(u4, lb_all[l], hgrn_out_gain[l], tc)
        w_up_p = jnp.concatenate(
            [gla_w_gate_up[l], jnp.zeros((LANE - GLA_GATE_RANK, H_C * DK_C), F32)], axis=0).astype(BF16)
        yc = _glac(u4, w_up_p, gla_b_gate[l], gla_out_gain[l], tc)

        x2 = _merge(ya, yb, yc, u3, w_branch[l].astype(BF16), w_out[l].astype(BF16), x2, gate[:, 1], seq)

        x2 = _ffn(x2, seq, shift[:, 2], scale[:, 2], gate[:, 2], gains[2], gains[3],
                  ffn_w_gate[l, 1].astype(BF16), ffn_w_up[l, 1].astype(BF16), ffn_w_down[l, 1].astype(BF16),
                  final_norm=True)
    return x2.reshape(bsz, seq, D_MODEL)
```

```python
import functools
import math

import jax
import jax.numpy as jnp
from jax import lax
from jax.experimental import pallas as pl
from jax.experimental.pallas import tpu as pltpu

D_MODEL = 2048
DEPTH = 2
CHUNK = 64
MIX_WIDTH = D_MODEL // 2
N_BRANCH = 3
H_A = 8
DH_A = 64
H_B = 8
DK_B = 128
DV_B = 128
H_C = 4
DK_C = 128
DV_C = 256
GLA_GATE_RANK = 16
GLA_GATE_NORM = 16.0
D_FF = 11 * D_MODEL // 4
N_BUCKETS = 32
MAX_DISTANCE = 128
EPS = 1e-6

LANE = 128
SUBLANES = 8
ROW_CHUNK = 256
MASK_VALUE = -1e30
VMEM_LIMIT = 56 * 1024 * 1024
N_GATE_BLK = N_BRANCH * D_MODEL // LANE
BLK_AQ = N_GATE_BLK
BLK_AK = BLK_AQ + H_A
BLK_AV = BLK_AK + H_A
BLK_BQ = BLK_AV + H_A
BLK_BF = BLK_BQ + H_B
BLK_BI = BLK_BF + H_B
BLK_BG = BLK_BI + H_B
BLK_CQ = BLK_BG + H_B
BLK_CK = BLK_CQ + H_C
BLK_CV = BLK_CK + H_C
BLK_CR = BLK_CV + 2 * H_C
BLK_CGD = BLK_CR + 2 * H_C
N_MAIN = 10 * MIX_WIDTH
IN_STEP_BLOCKS = 12
N_BLK = 132
N_IN_PAD = N_BLK * LANE

BF16 = jnp.bfloat16
F32 = jnp.float32


def _params(*sem):
    return pltpu.CompilerParams(dimension_semantics=sem, vmem_limit_bytes=VMEM_LIMIT)


def _sigmoid(x):
    return 1.0 / (1.0 + jnp.exp(-x))


def _log_sigmoid(x):
    return jnp.minimum(x, 0.0) - jnp.log(1.0 + jnp.exp(-jnp.abs(x)))


def _rms(x):
    return x * lax.rsqrt(jnp.mean(x * x, axis=-1, keepdims=True) + EPS)


def _dot(a, b):
    return jnp.dot(a, b, preferred_element_type=F32)


def _dot_nt(a, b):
    return lax.dot_general(a, b, (((1,), (1,)), ((), ())), preferred_element_type=F32)


def _dot_tn(a, b):
    return lax.dot_general(a, b, (((0,), (0,)), ((), ())), preferred_element_type=F32)


def _ada_kernel(c_ref, w_ref, b_ref, o_ref):
    c = c_ref[...]
    cond = c * _sigmoid(c)
    c_hi = cond.astype(BF16)
    c_lo = (cond - c_hi.astype(F32)).astype(BF16)
    w = w_ref[0]
    w_hi = w.astype(BF16)
    w_lo = (w - w_hi.astype(F32)).astype(BF16)
    o_ref[0] = _dot(c_hi, w_hi) + (_dot(c_lo, w_hi) + _dot(c_hi, w_lo)) + b_ref[0]


def _ada(c, w_ada, b_ada):
    bsz = c.shape[0]
    n = w_ada.shape[-1]
    tn = 1024
    return pl.pallas_call(
        _ada_kernel,
        grid=(DEPTH, n // tn),
        in_specs=[
            pl.BlockSpec((bsz, D_MODEL), lambda l, j: (0, 0)),
            pl.BlockSpec((1, D_MODEL, tn), lambda l, j: (l, 0, j)),
            pl.BlockSpec((1, 1, tn), lambda l, j: (l, 0, j)),
        ],
        out_specs=pl.BlockSpec((1, bsz, tn), lambda l, j: (l, 0, j)),
        out_shape=jax.ShapeDtypeStruct((DEPTH, bsz, n), F32),
        compiler_params=_params("parallel", "parallel"),
        name="ada_mod",
    )(c, w_ada, b_ada.reshape(DEPTH, 1, n))


def _modulated(x, gain, scale, shift):
    return (_rms(x) * gain) * (1.0 + scale) + shift


def _ffn_kernel(x_ref, shift_ref, scale_ref, gate_ref, gain_ref, fgain_ref, wg_ref, wu_ref, wd_ref,
                o_ref, h_ref, *, final_norm):
    j = pl.program_id(1)
    row_chunks = [slice(r, r + ROW_CHUNK) for r in range(0, x_ref.shape[0], ROW_CHUNK)]

    @pl.when(j == 0)
    def _():
        for rows in row_chunks:
            h = _modulated(x_ref[rows, :], gain_ref[...], scale_ref[0], shift_ref[0])
            h_ref[rows, :] = h.astype(BF16)
        o_ref[...] = jnp.zeros_like(o_ref)

    h = h_ref[...]
    g = _dot(h, wg_ref[...])
    u = _dot(h, wu_ref[...])
    a = (g * _sigmoid(g) * u).astype(BF16)
    o_ref[...] += _dot(a, wd_ref[...])

    @pl.when(j == pl.num_programs(1) - 1)
    def _():
        for rows in row_chunks:
            y = x_ref[rows, :] + 0.5 * gate_ref[0] * o_ref[rows, :]
            if final_norm:
                y = _rms(y) * fgain_ref[...]
            o_ref[rows, :] = y


def _ffn(x2, seq, shift, scale, gate, gain, fgain, wg, wu, wd, *, final_norm):
    t = x2.shape[0]
    tm = min(512, seq)
    tf = 512
    per_b = seq // tm
    row = pl.BlockSpec((tm, D_MODEL), lambda i, j: (i, 0))
    vec_b = pl.BlockSpec((1, 1, D_MODEL), lambda i, j: (i // per_b, 0, 0))
    vec = pl.BlockSpec((1, D_MODEL), lambda i, j: (0, 0))
    return pl.pallas_call(
        functools.partial(_ffn_kernel, final_norm=final_norm),
        grid=(t // tm, D_FF // tf),
        in_specs=[row, vec_b, vec_b, vec_b, vec, vec,
                  pl.BlockSpec((D_MODEL, tf), lambda i, j: (0, j)),
                  pl.BlockSpec((D_MODEL, tf), lambda i, j: (0, j)),
                  pl.BlockSpec((tf, D_MODEL), lambda i, j: (j, 0))],
        out_specs=row,
        out_shape=jax.ShapeDtypeStruct((t, D_MODEL), F32),
        scratch_shapes=[pltpu.VMEM((tm, D_MODEL), BF16)],
        compiler_params=_params("parallel", "arbitrary"),
        name="ffn",
    )(x2, shift, scale, gate, gain, fgain, wg, wu, wd)


def _inproj_kernel(x_ref, shift_ref, scale_ref, gain_ref, w_ref, o_ref, h_ref):
    @pl.when(pl.program_id(1) == 0)
    def _():
        for r in range(0, x_ref.shape[0], ROW_CHUNK):
            rows = slice(r, r + ROW_CHUNK)
            h = _modulated(x_ref[rows, :], gain_ref[...], scale_ref[0], shift_ref[0])
            h_ref[rows, :] = h.astype(BF16)

    res = _dot(h_ref[...], w_ref[...])
    for s in range(IN_STEP_BLOCKS):
        o_ref[s] = res[:, s * LANE:(s + 1) * LANE].astype(o_ref.dtype)


def _inproj(x2, seq, shift, scale, gain, w_in_p):
    t = x2.shape[0]
    tm = min(1024, seq)
    per_b = seq // tm
    tn = IN_STEP_BLOCKS * LANE
    return pl.pallas_call(
        _inproj_kernel,
        grid=(t // tm, N_BLK // IN_STEP_BLOCKS),
        in_specs=[
            pl.BlockSpec((tm, D_MODEL), lambda i, j: (i, 0)),
            pl.BlockSpec((1, 1, D_MODEL), lambda i, j: (i // per_b, 0, 0)),
            pl.BlockSpec((1, 1, D_MODEL), lambda i, j: (i // per_b, 0, 0)),
            pl.BlockSpec((1, D_MODEL), lambda i, j: (0, 0)),
            pl.BlockSpec((D_MODEL, tn), lambda i, j: (0, j)),
        ],
        out_specs=pl.BlockSpec((IN_STEP_BLOCKS, tm, LANE), lambda i, j: (j, i, 0)),
        out_shape=jax.ShapeDtypeStruct((N_BLK, t, LANE), BF16),
        scratch_shapes=[pltpu.VMEM((tm, D_MODEL), BF16)],
        compiler_params=_params("parallel", "arbitrary"),
        name="in_proj",
    )(x2, shift, scale, gain, w_in_p)


def _t5_bucket(rel):
    half = N_BUCKETS // 2
    max_exact = half // 2
    ret = jnp.where(rel > 0, half, 0)
    n = jnp.abs(rel)
    nf = jnp.maximum(n, 1).astype(F32)
    large = max_exact + (jnp.log(nf / max_exact) / math.log(MAX_DISTANCE / max_exact)
                         * (half - max_exact)).astype(jnp.int32)
    large = jnp.minimum(large, half - 1)
    return ret + jnp.where(n < max_exact, n, large)


def _far_bucket_is_constant(min_dist):
    half = N_BUCKETS // 2
    max_exact = half // 2
    val = math.log(min_dist / max_exact) / math.log(MAX_DISTANCE / max_exact) * (half - max_exact)
    return max_exact + val >= half


def _bias_tiles(rel_bias, tq):
    assert _far_bucket_is_constant(tq + 1)
    kpos = jnp.arange(tq)[:, None]
    qpos = jnp.arange(tq)[None, :]
    table = rel_bias.astype(F32) - rel_bias[N_BUCKETS // 2 - 1].astype(F32)

    def lookup(bucket):
        out = jnp.zeros((H_A, tq, tq), F32)
        for n in range(N_BUCKETS):
            out = jnp.where(bucket[None] == n, table[n][:, None, None], out)
        return out

    b0 = lookup(_t5_bucket(kpos - qpos))
    b1 = lookup(_t5_bucket(kpos - tq - qpos))
    visible = (kpos // CHUNK) <= (qpos // CHUNK)
    return jnp.where(visible[None], b0, MASK_VALUE), b1


ONES_ROWS = 16
ATTN_HEADS_PER_STEP = 2


def _qk_norm(x, gain):
    low = lax.broadcasted_iota(jnp.int32, (1, LANE), 1) < DH_A
    sq = x * x
    lo = jnp.sum(jnp.where(low, sq, 0.0), axis=-1, keepdims=True)
    hi = jnp.sum(jnp.where(low, 0.0, sq), axis=-1, keepdims=True)
    ms = jnp.where(low, lo, hi) * (1.0 / DH_A)
    return x * lax.rsqrt(ms + EPS) * gain


def _attn_kernel(lv_ref, og_ref, qg_ref, kg_ref, q_ref, k_ref, v_ref, b0_ref, b1_ref, o_ref,
                 kn_ref, vt_ref, qst_ref, sa_ref, sb_ref, m_ref, acc_ref, *, tq, lam_init):
    qi = pl.program_id(2)
    seq = k_ref.shape[2]

    heads = range(ATTN_HEADS_PER_STEP)

    @pl.when(qi == 0)
    def _():
        for hh in heads:
            for blk in range(seq // tq):
                rows = slice(blk * tq, (blk + 1) * tq)
                kn_ref[hh, rows, :] = _qk_norm(
                    k_ref[hh, 0, rows, :].astype(F32), kg_ref[...]).astype(kn_ref.dtype)
                vt_ref[hh, 0:LANE, rows] = v_ref[hh, 0, rows, :].astype(F32).T.astype(vt_ref.dtype)
            vt_ref[hh, LANE:, :] = jnp.ones((ONES_ROWS, seq), vt_ref.dtype)

    sub = lax.broadcasted_iota(jnp.int32, (LANE, 1), 0)
    for hh in heads:
        qt = _qk_norm(q_ref[hh, 0].astype(F32), qg_ref[...]).T
        qst_ref[hh] = jnp.concatenate(
            [jnp.where(sub < DH_A, qt, 0.0), jnp.where(sub < DH_A, 0.0, qt)], axis=1).astype(qst_ref.dtype)

    m_ref[...] = jnp.full_like(m_ref, MASK_VALUE)
    acc_ref[...] = jnp.zeros_like(acc_ref)

    def logits(blk, bias_ref, dst_ref):
        for hh in heads:
            s = _dot(kn_ref[hh, pl.ds(pl.multiple_of(blk * tq, tq), tq), :], qst_ref[hh])
            if bias_ref is not None:
                bias = bias_ref[hh]
                s = s + jnp.concatenate([bias, bias], axis=1)
            dst_ref[hh] = s

    def accumulate(src_ref, blk):
        for hh in heads:
            s = src_ref[hh]
            vtb = vt_ref[hh, :, pl.ds(pl.multiple_of(blk * tq, tq), tq)]
            m_prev = m_ref[hh, 0:1, :]
            m_new = jnp.maximum(m_prev, jnp.max(s, axis=0, keepdims=True))
            alpha = jnp.exp(m_prev - m_new)
            p = jnp.exp(s - m_new).astype(BF16)
            m_ref[hh] = jnp.broadcast_to(m_new, m_ref.shape[1:])
            acc_ref[hh] = alpha * acc_ref[hh] + _dot(vtb, p)

    n_far = jnp.maximum(qi - 1, 0)
    logits(0, None, sa_ref)

    def far_pair(i, carry):
        blk = 2 * i
        logits(blk + 1, None, sb_ref)
        accumulate(sa_ref, blk)
        logits(jnp.minimum(blk + 2, n_far - 1), None, sa_ref)
        accumulate(sb_ref, blk + 1)
        return carry

    lax.fori_loop(0, n_far // 2, far_pair, 0)

    @pl.when(n_far % 2 == 1)
    def _():
        accumulate(sa_ref, n_far - 1)

    @pl.when(qi >= 1)
    def _():
        logits(qi - 1, b1_ref, sa_ref)
        logits(qi, b0_ref, sb_ref)
        accumulate(sa_ref, qi - 1)
        accumulate(sb_ref, qi)

    @pl.when(qi == 0)
    def _():
        logits(0, b0_ref, sb_ref)
        accumulate(sb_ref, 0)

    lv = lv_ref[...]
    lam = (jnp.exp(jnp.sum(lv[0:1] * lv[1:2], axis=-1, keepdims=True))
           - jnp.exp(jnp.sum(lv[2:3] * lv[3:4], axis=-1, keepdims=True)) + lam_init)
    for hh in heads:
        acc = acc_ref[hh]
        ot = acc[0:LANE] * (1.0 / acc[LANE:LANE + 1])
        odt = ot[:, :tq] - lam * ot[:, tq:]
        yt = odt * lax.rsqrt(jnp.mean(odt * odt, axis=0, keepdims=True) + EPS)
        o_ref[:, hh * LANE:(hh + 1) * LANE] = (yt.T * og_ref[...] * (1.0 - lam_init)).astype(o_ref.dtype)


def _attention(u4, lam_vec, out_gain, qk_gain, bias0, bias1, *, lam_init):
    _, bsz, seq, _ = u4.shape
    tq = bias0.shape[-1]
    nq = seq // tq
    hp = ATTN_HEADS_PER_STEP
    kv_spec = lambda base: pl.BlockSpec((hp, 1, seq, LANE), lambda b, h, i: (base // hp + h, b, 0, 0))
    bias_spec = pl.BlockSpec((hp, tq, tq), lambda b, h, i: (h, 0, 0))
    vec = pl.BlockSpec((1, LANE), lambda b, h, i: (0, 0))
    q_gain = jnp.tile(qk_gain[0] * (DH_A ** -0.5), 2).reshape(1, LANE)
    k_gain = jnp.tile(qk_gain[1], 2).reshape(1, LANE)
    return pl.pallas_call(
        functools.partial(_attn_kernel, tq=tq, lam_init=lam_init),
        grid=(bsz, H_A // hp, nq),
        in_specs=[
            pl.BlockSpec((4, DH_A), lambda b, h, i: (0, 0)),
            vec, vec, vec,
            pl.BlockSpec((hp, 1, tq, LANE), lambda b, h, i: (BLK_AQ // hp + h, b, i, 0)),
            kv_spec(BLK_AK), kv_spec(BLK_AV), bias_spec, bias_spec,
        ],
        out_specs=pl.BlockSpec((tq, hp * LANE), lambda b, h, i: (b * nq + i, h)),
        out_shape=jax.ShapeDtypeStruct((bsz * seq, MIX_WIDTH), BF16),
        scratch_shapes=[pltpu.VMEM((hp, seq, LANE), BF16),
                        pltpu.VMEM((hp, LANE + ONES_ROWS, seq), BF16),
                        pltpu.VMEM((hp, LANE, 2 * tq), BF16),
                        pltpu.VMEM((hp, tq, 2 * tq), F32),
                        pltpu.VMEM((hp, tq, 2 * tq), F32),
                        pltpu.VMEM((hp, 8, 2 * tq), F32),
                        pltpu.VMEM((hp, LANE + ONES_ROWS, 2 * tq), F32)],
        compiler_params=_params("parallel", "parallel", "arbitrary"),
        name="diff_attn",
    )(lam_vec, out_gain.reshape(1, LANE), q_gain, k_gain, u4, u4, u4, bias0, bias1)


GLA_TILE = 2 * CHUNK
HGRN_HEADS_PER_STEP = 4
GLA_HEADS_PER_STEP = 4


def _gla_block(q, k, v, g, st_ref):
    r_rows, dk = q.shape
    n8 = r_rows // SUBLANES
    n_chunks = r_rows // CHUNK
    per_chunk = CHUNK // SUBLANES
    sub = lax.broadcasted_iota(jnp.int32, (1, SUBLANES, dk), 1)
    q3, k3, g3 = (a.reshape(n8, SUBLANES, dk) for a in (q, k, g))

    p = g3
    sh = 1
    while sh < SUBLANES:
        p = p + jnp.where(sub >= sh, pltpu.roll(p, sh, axis=1), 0.0)
        sh *= 2
    p4 = p.reshape(n_chunks, per_chunk, SUBLANES, dk)
    total = p4[:, 0, SUBLANES - 1:, :]
    groups = [p4[:, 0]]
    for j in range(1, per_chunk):
        groups.append(p4[:, j] + total)
        total = total + p4[:, j, SUBLANES - 1:, :]
    b4 = jnp.stack(groups, axis=1)
    b3 = b4.reshape(n8, SUBLANES, dk)

    levels = []
    w = CHUNK // 2
    while w >= SUBLANES:
        m = w // SUBLANES
        pairs = (n8 // (2 * m), 2, m, SUBLANES, dk)
        b5, q5, k5 = b3.reshape(pairs), q3.reshape(pairs), k3.reshape(pairs)
        ref = b5[:, 0:1, m - 1:, SUBLANES - 1:, :]
        d = jnp.concatenate([ref - b5[:, 0:1], b5[:, 1:2] - ref], axis=1)
        x = jnp.concatenate([k5[:, 0:1], q5[:, 1:2]], axis=1) * jnp.exp(d)
        levels.append((w, x.reshape(r_rows, dk).astype(BF16)))
        w //= 2
    while w >= 1:
        odd = (sub & w) != 0
        if w == 4:
            ref = b3[:, 3:4, :]
            d = jnp.where(odd, b3 - ref, ref - b3)
        elif w == 2:
            ref = jnp.where(sub < 4, b3[:, 1:2, :], b3[:, 5:6, :])
            d = jnp.where(odd, b3 - ref, ref - b3)
        else:
            d = jnp.where(odd, g3, 0.0)
        x = jnp.where(odd, q3, k3) * jnp.exp(d)
        levels.append((w, x.reshape(r_rows, dk).astype(BF16)))
        w //= 2
    qb = q.astype(BF16)
    kb = k.astype(BF16)

    ti = lax.broadcasted_iota(jnp.int32, (GLA_TILE, 1), 0)
    si = lax.broadcasted_iota(jnp.int32, (1, GLA_TILE), 1)
    lev = jnp.where(si < ti, jnp.bitwise_xor(ti, si), 0)
    masks = [(lev >= w) & (lev < 2 * w) for w, _ in levels]
    diag = ti == si

    intra = []
    for sb in range(r_rows // GLA_TILE):
        sl = slice(sb * GLA_TILE, (sb + 1) * GLA_TILE)
        sc = jnp.where(diag, _dot_nt(qb[sl], kb[sl]), 0.0)
        for (w, x), mk in zip(levels, masks):
            sc = jnp.where(mk, _dot_nt(x[sl], x[sl]), sc)
        intra.append(_dot(sc.astype(BF16), v[sl]))

    chunked = (n_chunks, per_chunk, SUBLANES, dk)
    b_last = total[:, None]
    qe = (q3.reshape(chunked) * jnp.exp(b4)).reshape(r_rows, dk).astype(BF16)
    ke = (k3.reshape(chunked) * jnp.exp(b_last - b4)).reshape(r_rows, dk).astype(BF16)
    dec = jnp.exp(total)
    st = st_ref[...]
    inter = []
    for c in range(n_chunks):
        sl = slice(c * CHUNK, (c + 1) * CHUNK)
        inter.append(_dot_nt(qe[sl], st.astype(BF16)))
        st = st * dec[c] + _dot_tn(v[sl], ke[sl])
    st_ref[...] = st
    return jnp.concatenate(intra, axis=0) + jnp.concatenate(inter, axis=0)


def _hgrn_kernel(lb_ref, og_ref, q_ref, f_ref, i_ref, g_ref, o_ref, st_ref):
    @pl.when(pl.program_id(2) == 0)
    def _():
        st_ref[...] = jnp.zeros_like(st_ref)

    for hh in range(HGRN_HEADS_PER_STEP):
        lb = lb_ref[hh]
        zf = f_ref[hh, 0].astype(F32)
        t = jnp.exp(-jnp.abs(zf))
        inv = 1.0 / (1.0 + t)
        pos = zf >= 0.0
        sig = jnp.where(pos, 1.0, t) * inv
        log_f = jnp.where(lb > 0.0, jnp.log(lb + (1.0 - lb) * sig), jnp.minimum(zf, 0.0) - jnp.log(1.0 + t))
        k = (1.0 - lb) * (jnp.where(pos, t, 1.0) * inv)
        o = _gla_block(q_ref[hh, 0].astype(F32), k, i_ref[hh, 0], log_f, st_ref.at[hh])
        o = o * _sigmoid(g_ref[hh, 0].astype(F32))
        o_ref[:, hh * DV_B:(hh + 1) * DV_B] = (_rms(o) * og_ref[...]).astype(o_ref.dtype)


def _hgrn(u4, lb, out_gain, tc):
    _, bsz, seq, _ = u4.shape
    nt = seq // tc
    hp = HGRN_HEADS_PER_STEP
    blk = lambda base: pl.BlockSpec((hp, 1, tc, LANE), lambda b, h, i: (base // hp + h, b, i, 0))
    return pl.pallas_call(
        _hgrn_kernel,
        grid=(bsz, H_B // hp, nt),
        in_specs=[
            pl.BlockSpec((hp, 1, DK_B), lambda b, h, i: (h, 0, 0)),
            pl.BlockSpec((1, DV_B), lambda b, h, i: (0, 0)),
            blk(BLK_BQ), blk(BLK_BF), blk(BLK_BI), blk(BLK_BG),
        ],
        out_specs=pl.BlockSpec((tc, hp * DV_B), lambda b, h, i: (b * nt + i, h)),
        out_shape=jax.ShapeDtypeStruct((bsz * seq, MIX_WIDTH), BF16),
        scratch_shapes=[pltpu.VMEM((hp, DV_B, DK_B), F32)],
        compiler_params=_params("parallel", "parallel", "arbitrary"),
        name="hgrn2",
    )(lb.reshape(H_B, 1, DK_B), out_gain.reshape(1, DV_B), u4, u4, u4, u4)


def _glac_kernel(wup_ref, bup_ref, og_ref, q_ref, k_ref, v_ref, r_ref, gd_ref, o_ref, st_ref):
    @pl.when(pl.program_id(2) == 0)
    def _():
        st_ref[...] = jnp.zeros_like(st_ref)

    for hh in range(GLA_HEADS_PER_STEP):
        z = _dot(gd_ref[0, 0], wup_ref[:, hh * DK_C:(hh + 1) * DK_C]) + bup_ref[hh]
        log_a = _log_sigmoid(z) * (1.0 / GLA_GATE_NORM)
        q = q_ref[hh, 0].astype(F32) * (DK_C ** -0.5)
        v = jnp.concatenate([v_ref[2 * hh, 0], v_ref[2 * hh + 1, 0]], axis=1)
        o = _gla_block(q, k_ref[hh, 0].astype(F32), v, log_a, st_ref.at[hh])
        gate = jnp.concatenate([r_ref[2 * hh, 0], r_ref[2 * hh + 1, 0]], axis=1).astype(F32)
        o_ref[:, hh * DV_C:(hh + 1) * DV_C] = (
            _rms(o) * og_ref[...] * (gate * _sigmoid(gate))).astype(o_ref.dtype)


def _glac(u4, w_up_p, b_up, out_gain, tc):
    _, bsz, seq, _ = u4.shape
    nt = seq // tc
    hp = GLA_HEADS_PER_STEP
    one = lambda base: pl.BlockSpec((hp, 1, tc, LANE), lambda b, h, i: (base // hp + h, b, i, 0))
    two = lambda base: pl.BlockSpec((2 * hp, 1, tc, LANE), lambda b, h, i: (base // (2 * hp) + h, b, i, 0))
    return pl.pallas_call(
        _glac_kernel,
        grid=(bsz, H_C // hp, nt),
        in_specs=[
            pl.BlockSpec((LANE, hp * DK_C), lambda b, h, i: (0, h)),
            pl.BlockSpec((hp, 1, DK_C), lambda b, h, i: (h, 0, 0)),
            pl.BlockSpec((1, DV_C), lambda b, h, i: (0, 0)),
            one(BLK_CQ), one(BLK_CK), two(BLK_CV), two(BLK_CR),
            pl.BlockSpec((1, 1, tc, LANE), lambda b, h, i: (BLK_CGD, b, i, 0)),
        ],
        out_specs=pl.BlockSpec((tc, hp * DV_C), lambda b, h, i: (b * nt + i, h)),
        out_shape=jax.ShapeDtypeStruct((bsz * seq, MIX_WIDTH), BF16),
        scratch_shapes=[pltpu.VMEM((hp, DV_C, DK_C), F32)],
        compiler_params=_params("parallel", "parallel", "arbitrary"),
        name="gla",
    )(w_up_p, b_up.reshape(H_C, 1, DK_C), out_gain.reshape(1, DV_C), u4, u4, u4, u4, u4)


def _merge_kernel(ya_ref, yb_ref, yc_ref, gl_ref, wb_ref, wo_ref, x_ref, gate_ref, o_ref):
    n = pl.program_id(1)
    m_ref = o_ref

    def contribution(y_ref, branch):
        z = _dot(y_ref[...], wb_ref[branch])
        cols = []
        for c in range(D_MODEL // LANE):
            cols.append(_sigmoid(gl_ref[c].astype(F32)) * z[:, c * LANE:(c + 1) * LANE])
        return jnp.concatenate(cols, axis=1)

    @pl.when(n == 0)
    def _():
        m_ref[...] = contribution(ya_ref, 0)

    @pl.when(n == 1)
    def _():
        m_ref[...] += contribution(yb_ref, 1)

    @pl.when(n == 2)
    def _():
        m = m_ref[...] + contribution(yc_ref, 2)
        o_ref[...] = x_ref[...] + gate_ref[0] * _dot(m.astype(BF16), wo_ref[...])


def _merge(ya, yb, yc, u3, w_branch, w_out, x2, gate, seq):
    t = x2.shape[0]
    tm = min(512, seq)
    per_b = seq // tm
    ysp = pl.BlockSpec((tm, MIX_WIDTH), lambda i, n: (i, 0))
    row = pl.BlockSpec((tm, D_MODEL), lambda i, n: (i, 0))
    return pl.pallas_call(
        _merge_kernel,
        grid=(t // tm, N_BRANCH),
        in_specs=[
            ysp, ysp, ysp,
            pl.BlockSpec((D_MODEL // LANE, tm, LANE), lambda i, n: (n, i, 0)),
            pl.BlockSpec((N_BRANCH, MIX_WIDTH, D_MODEL), lambda i, n: (0, 0, 0), pipeline_mode=pl.Buffered(1)),
            pl.BlockSpec((D_MODEL, D_MODEL), lambda i, n: (0, 0), pipeline_mode=pl.Buffered(1)),
            row,
            pl.BlockSpec((1, 1, D_MODEL), lambda i, n: (i // per_b, 0, 0)),
        ],
        out_specs=row,
        out_shape=jax.ShapeDtypeStruct((t, D_MODEL), F32),
        compiler_params=_params("parallel", "arbitrary"),
        name="merge_out",
    )(ya, yb, yc, u3, w_branch, w_out, x2, gate)


def kernel(x, c, w_ada, b_ada, norm_gains, ffn_w_gate, ffn_w_up, ffn_w_down, w_in, qk_gains, diff_lambda,
           diff_out_gain, rel_bias, hgrn_lb_logits, hgrn_out_gain, gla_w_gate_up, gla_b_gate, gla_out_gain,
           w_branch, w_out):
    bsz, seq, _ = x.shape
    t = bsz * seq
    tq = min(512, seq)
    tc = min(512, seq)

    lb_all = jnp.cumsum(jax.nn.softmax(hgrn_lb_logits.astype(F32), axis=0), axis=0)
    lb_all = lb_all - lb_all[0]
    mod = _ada(c, w_ada, b_ada).reshape(DEPTH, bsz, 3, 3, 1, D_MODEL)
    bias0, bias1 = _bias_tiles(rel_bias, tq)

    x2 = x.reshape(t, D_MODEL)
    for l in range(DEPTH):
        shift, scale, gate = mod[l, :, :, 0], mod[l, :, :, 1], mod[l, :, :, 2]
        gains = norm_gains[l].reshape(4, 1, D_MODEL)
        lam_init = 0.8 - 0.6 * math.exp(-0.3 * l)

        x2 = _ffn(x2, seq, shift[:, 0], scale[:, 0], gate[:, 0], gains[0], gains[3],
                  ffn_w_gate[l, 0].astype(BF16), ffn_w_up[l, 0].astype(BF16), ffn_w_down[l, 0].astype(BF16),
                  final_norm=False)

        w_l = w_in[l]
        n_real = N_MAIN + GLA_GATE_RANK
        w_in_p = jnp.concatenate(
            [w_l[:, n_real:].astype(BF16), w_l[:, :n_real].astype(BF16),
             jnp.zeros((D_MODEL, N_IN_PAD - w_l.shape[1]), BF16)], axis=1)
        u3 = _inproj(x2, seq, shift[:, 1], scale[:, 1], gains[1], w_in_p)
        u4 = u3.reshape(N_BLK, bsz, seq, LANE)

        ya = _attention(u4, diff_lambda[l], diff_out_gain[l], qk_gains[l], bias0, bias1, lam_init=lam_init)
        yb = _hgrn(u4, lb_all[l], hgrn_out_gain[l], tc)
        w_up_p = jnp.concatenate(
            [gla_w_gate_up[l], jnp.zeros((LANE - GLA_GATE_RANK, H_C * DK_C), F32)], axis=0).astype(BF16)
        yc = _glac(u4, w_up_p, gla_b_gate[l], gla_out_gain[l], tc)

        x2 = _merge(ya, yb, yc, u3, w_branch[l].astype(BF16), w_out[l].astype(BF16), x2, gate[:, 1], seq)

        x2 = _ffn(x2, seq, shift[:, 2], scale[:, 2], gate[:, 2], gains[2], gains[3],
                  ffn_w_gate[l, 1].astype(BF16), ffn_w_up[l, 1].astype(BF16), ffn_w_down[l, 1].astype(BF16),
                  final_norm=True)
    return x2.reshape(bsz, seq, D_MODEL)
```

```python
import functools
import math

import jax
import jax.numpy as jnp
from jax import lax
from jax.experimental import pallas as pl
from jax.experimental.pallas import tpu as pltpu

D_MODEL = 2048
DEPTH = 2
CHUNK = 64
MIX_WIDTH = D_MODEL // 2
N_BRANCH = 3
H_A = 8
DH_A = 64
H_B = 8
DK_B = 128
DV_B = 128
H_C = 4
DK_C = 128
DV_C = 256
GLA_GATE_RANK = 16
GLA_GATE_NORM = 16.0
D_FF = 11 * D_MODEL // 4
N_BUCKETS = 32
MAX_DISTANCE = 128
EPS = 1e-6

LANE = 128
SUBLANES = 8
ROW_CHUNK = 256
MASK_VALUE = -1e30
LOG2_E = math.log2(math.e)
VMEM_LIMIT = 56 * 1024 * 1024
N_GATE_BLK = N_BRANCH * D_MODEL // LANE
BLK_AQ = N_GATE_BLK
BLK_AK = BLK_AQ + H_A
BLK_AV = BLK_AK + H_A
BLK_BQ = BLK_AV + H_A
BLK_BF = BLK_BQ + H_B
BLK_BI = BLK_BF + H_B
BLK_BG = BLK_BI + H_B
BLK_CQ = BLK_BG + H_B
BLK_CK = BLK_CQ + H_C
BLK_CV = BLK_CK + H_C
BLK_CR = BLK_CV + 2 * H_C
BLK_CGD = BLK_CR + 2 * H_C
N_MAIN = 10 * MIX_WIDTH
IN_STEP_BLOCKS = 12
N_BLK = 132
N_IN_PAD = N_BLK * LANE

BF16 = jnp.bfloat16
F32 = jnp.float32


def _params(*sem):
    return pltpu.CompilerParams(dimension_semantics=sem, vmem_limit_bytes=VMEM_LIMIT)


def _sigmoid(x):
    return 1.0 / (1.0 + jnp.exp(-x))


def _log_sigmoid(x):
    return jnp.minimum(x, 0.0) - jnp.log(1.0 + jnp.exp(-jnp.abs(x)))


def _rms(x):
    return x * lax.rsqrt(jnp.mean(x * x, axis=-1, keepdims=True) + EPS)


def _dot(a, b):
    return jnp.dot(a, b, preferred_element_type=F32)


def _dot_nt(a, b):
    return lax.dot_general(a, b, (((1,), (1,)), ((), ())), preferred_element_type=F32)


def _dot_tn(a, b):
    return lax.dot_general(a, b, (((0,), (0,)), ((), ())), preferred_element_type=F32)


def _ada_kernel(c_ref, w_ref, b_ref, o_ref):
    c = c_ref[...]
    cond = c * _sigmoid(c)
    c_hi = cond.astype(BF16)
    c_lo = (cond - c_hi.astype(F32)).astype(BF16)
    w = w_ref[0]
    w_hi = w.astype(BF16)
    w_lo = (w - w_hi.astype(F32)).astype(BF16)
    o_ref[0] = _dot(c_hi, w_hi) + (_dot(c_lo, w_hi) + _dot(c_hi, w_lo)) + b_ref[0]


def _ada(c, w_ada, b_ada):
    bsz = c.shape[0]
    n = w_ada.shape[-1]
    tn = 1024
    return pl.pallas_call(
        _ada_kernel,
        grid=(DEPTH, n // tn),
        in_specs=[
            pl.BlockSpec((bsz, D_MODEL), lambda l, j: (0, 0)),
            pl.BlockSpec((1, D_MODEL, tn), lambda l, j: (l, 0, j)),
            pl.BlockSpec((1, 1, tn), lambda l, j: (l, 0, j)),
        ],
        out_specs=pl.BlockSpec((1, bsz, tn), lambda l, j: (l, 0, j)),
        out_shape=jax.ShapeDtypeStruct((DEPTH, bsz, n), F32),
        compiler_params=_params("parallel", "parallel"),
        name="ada_mod",
    )(c, w_ada, b_ada.reshape(DEPTH, 1, n))


def _modulated(x, gain, scale, shift):
    return (_rms(x) * gain) * (1.0 + scale) + shift


def _ffn_kernel(x_ref, shift_ref, scale_ref, gate_ref, gain_ref, fgain_ref, wg_ref, wu_ref, wd_ref,
                o_ref, h_ref, *, final_norm):
    j = pl.program_id(1)
    row_chunks = [slice(r, r + ROW_CHUNK) for r in range(0, x_ref.shape[0], ROW_CHUNK)]

    @pl.when(j == 0)
    def _():
        for rows in row_chunks:
            h = _modulated(x_ref[rows, :], gain_ref[...], scale_ref[0], shift_ref[0])
            h_ref[rows, :] = h.astype(BF16)
        o_ref[...] = jnp.zeros_like(o_ref)

    h = h_ref[...]
    g = _dot(h, wg_ref[...])
    u = _dot(h, wu_ref[...])
    a = (g * _sigmoid(g) * u).astype(BF16)
    o_ref[...] += _dot(a, wd_ref[...])

    @pl.when(j == pl.num_programs(1) - 1)
    def _():
        for rows in row_chunks:
            y = x_ref[rows, :] + 0.5 * gate_ref[0] * o_ref[rows, :]
            if final_norm:
                y = _rms(y) * fgain_ref[...]
            o_ref[rows, :] = y


def _ffn(x2, seq, shift, scale, gate, gain, fgain, wg, wu, wd, *, final_norm):
    t = x2.shape[0]
    tm = min(512, seq)
    tf = 512
    per_b = seq // tm
    row = pl.BlockSpec((tm, D_MODEL), lambda i, j: (i, 0))
    vec_b = pl.BlockSpec((1, 1, D_MODEL), lambda i, j: (i // per_b, 0, 0))
    vec = pl.BlockSpec((1, D_MODEL), lambda i, j: (0, 0))
    return pl.pallas_call(
        functools.partial(_ffn_kernel, final_norm=final_norm),
        grid=(t // tm, D_FF // tf),
        in_specs=[row, vec_b, vec_b, vec_b, vec, vec,
                  pl.BlockSpec((D_MODEL, tf), lambda i, j: (0, j)),
                  pl.BlockSpec((D_MODEL, tf), lambda i, j: (0, j)),
                  pl.BlockSpec((tf, D_MODEL), lambda i, j: (j, 0))],
        out_specs=row,
        out_shape=jax.ShapeDtypeStruct((t, D_MODEL), F32),
        scratch_shapes=[pltpu.VMEM((tm, D_MODEL), BF16)],
        compiler_params=_params("parallel", "arbitrary"),
        name="ffn",
    )(x2, shift, scale, gate, gain, fgain, wg, wu, wd)


def _inproj_kernel(x_ref, shift_ref, scale_ref, gain_ref, w_ref, o_ref, h_ref):
    @pl.when(pl.program_id(1) == 0)
    def _():
        for r in range(0, x_ref.shape[0], ROW_CHUNK):
            rows = slice(r, r + ROW_CHUNK)
            h = _modulated(x_ref[rows, :], gain_ref[...], scale_ref[0], shift_ref[0])
            h_ref[rows, :] = h.astype(BF16)

    res = _dot(h_ref[...], w_ref[...])
    for s in range(IN_STEP_BLOCKS):
        o_ref[s] = res[:, s * LANE:(s + 1) * LANE].astype(o_ref.dtype)


def _inproj(x2, seq, shift, scale, gain, w_in_p):
    t = x2.shape[0]
    tm = min(1024, seq)
    per_b = seq // tm
    tn = IN_STEP_BLOCKS * LANE
    return pl.pallas_call(
        _inproj_kernel,
        grid=(t // tm, N_BLK // IN_STEP_BLOCKS),
        in_specs=[
            pl.BlockSpec((tm, D_MODEL), lambda i, j: (i, 0)),
            pl.BlockSpec((1, 1, D_MODEL), lambda i, j: (i // per_b, 0, 0)),
            pl.BlockSpec((1, 1, D_MODEL), lambda i, j: (i // per_b, 0, 0)),
            pl.BlockSpec((1, D_MODEL), lambda i, j: (0, 0)),
            pl.BlockSpec((D_MODEL, tn), lambda i, j: (0, j)),
        ],
        out_specs=pl.BlockSpec((IN_STEP_BLOCKS, tm, LANE), lambda i, j: (j, i, 0)),
        out_shape=jax.ShapeDtypeStruct((N_BLK, t, LANE), BF16),
        scratch_shapes=[pltpu.VMEM((tm, D_MODEL), BF16)],
        compiler_params=_params("parallel", "arbitrary"),
        name="in_proj",
    )(x2, shift, scale, gain, w_in_p)


def _t5_bucket(rel):
    half = N_BUCKETS // 2
    max_exact = half // 2
    ret = jnp.where(rel > 0, half, 0)
    n = jnp.abs(rel)
    nf = jnp.maximum(n, 1).astype(F32)
    large = max_exact + (jnp.log(nf / max_exact) / math.log(MAX_DISTANCE / max_exact)
                         * (half - max_exact)).astype(jnp.int32)
    large = jnp.minimum(large, half - 1)
    return ret + jnp.where(n < max_exact, n, large)


def _far_bucket_is_constant(min_dist):
    half = N_BUCKETS // 2
    max_exact = half // 2
    val = math.log(min_dist / max_exact) / math.log(MAX_DISTANCE / max_exact) * (half - max_exact)
    return max_exact + val >= half


def _bias_tiles(rel_bias, tq):
    assert _far_bucket_is_constant(tq + 1)
    kpos = jnp.arange(tq)[:, None]
    qpos = jnp.arange(tq)[None, :]
    table = (rel_bias.astype(F32) - rel_bias[N_BUCKETS // 2 - 1].astype(F32)) * LOG2_E

    def lookup(bucket):
        out = jnp.zeros((H_A, tq, tq), F32)
        for n in range(N_BUCKETS):
            out = jnp.where(bucket[None] == n, table[n][:, None, None], out)
        return out

    b0 = lookup(_t5_bucket(kpos - qpos))
    b1 = lookup(_t5_bucket(kpos - tq - qpos))
    visible = (kpos // CHUNK) <= (qpos // CHUNK)
    return jnp.where(visible[None], b0, MASK_VALUE), b1


ONES_ROWS = 16
ATTN_HEADS_PER_STEP = 2


def _qk_norm(x, gain):
    low = lax.broadcasted_iota(jnp.int32, (1, LANE), 1) < DH_A
    sq = x * x
    lo = jnp.sum(jnp.where(low, sq, 0.0), axis=-1, keepdims=True)
    hi = jnp.sum(jnp.where(low, 0.0, sq), axis=-1, keepdims=True)
    ms = jnp.where(low, lo, hi) * (1.0 / DH_A)
    return x * lax.rsqrt(ms + EPS) * gain


def _attn_kernel(lv_ref, og_ref, qg_ref, kg_ref, q_ref, k_ref, v_ref, b0_ref, b1_ref, o_ref,
                 kn_ref, vt_ref, qst_ref, sa_ref, sb_ref, m_ref, acc_ref, *, tq, lam_init):
    qi = pl.program_id(2)
    seq = k_ref.shape[2]

    heads = range(ATTN_HEADS_PER_STEP)

    @pl.when(qi == 0)
    def _():
        for hh in heads:
            for blk in range(seq // tq):
                rows = slice(blk * tq, (blk + 1) * tq)
                kn_ref[hh, rows, :] = _qk_norm(
                    k_ref[hh, 0, rows, :].astype(F32), kg_ref[...]).astype(kn_ref.dtype)
                vt_ref[hh, 0:LANE, rows] = v_ref[hh, 0, rows, :].astype(F32).T.astype(vt_ref.dtype)
            vt_ref[hh, LANE:, :] = jnp.ones((ONES_ROWS, seq), vt_ref.dtype)

    sub = lax.broadcasted_iota(jnp.int32, (LANE, 1), 0)
    for hh in heads:
        qt = _qk_norm(q_ref[hh, 0].astype(F32), qg_ref[...]).T
        qst_ref[hh] = jnp.concatenate(
            [jnp.where(sub < DH_A, qt, 0.0), jnp.where(sub < DH_A, 0.0, qt)], axis=1).astype(qst_ref.dtype)

    m_ref[...] = jnp.full_like(m_ref, MASK_VALUE)
    acc_ref[...] = jnp.zeros_like(acc_ref)

    def logits(blk, bias_ref, dst_ref):
        for hh in heads:
            s = _dot(kn_ref[hh, pl.ds(pl.multiple_of(blk * tq, tq), tq), :], qst_ref[hh])
            if bias_ref is not None:
                bias = bias_ref[hh]
                s = s + jnp.concatenate([bias, bias], axis=1)
            dst_ref[hh] = s

    def accumulate(src_ref, blk):
        for hh in heads:
            s = src_ref[hh]
            vtb = vt_ref[hh, :, pl.ds(pl.multiple_of(blk * tq, tq), tq)]
            m_prev = m_ref[hh, 0:1, :]
            m_new = jnp.maximum(m_prev, jnp.max(s, axis=0, keepdims=True))
            alpha = jnp.exp2(m_prev - m_new)
            p = jnp.exp2(s - m_new).astype(BF16)
            m_ref[hh] = jnp.broadcast_to(m_new, m_ref.shape[1:])
            acc_ref[hh] = alpha * acc_ref[hh] + _dot(vtb, p)

    n_far = jnp.maximum(qi - 1, 0)
    logits(0, None, sa_ref)

    def far_pair(i, carry):
        blk = 2 * i
        logits(blk + 1, None, sb_ref)
        accumulate(sa_ref, blk)
        logits(jnp.minimum(blk + 2, n_far - 1), None, sa_ref)
        accumulate(sb_ref, blk + 1)
        return carry

    lax.fori_loop(0, n_far // 2, far_pair, 0)

    @pl.when(n_far % 2 == 1)
    def _():
        accumulate(sa_ref, n_far - 1)

    @pl.when(qi >= 1)
    def _():
        logits(qi - 1, b1_ref, sa_ref)
        logits(qi, b0_ref, sb_ref)
        accumulate(sa_ref, qi - 1)
        accumulate(sb_ref, qi)

    @pl.when(qi == 0)
    def _():
        logits(0, b0_ref, sb_ref)
        accumulate(sb_ref, 0)

    lv = lv_ref[...]
    lam = (jnp.exp(jnp.sum(lv[0:1] * lv[1:2], axis=-1, keepdims=True))
           - jnp.exp(jnp.sum(lv[2:3] * lv[3:4], axis=-1, keepdims=True)) + lam_init)
    for hh in heads:
        acc = acc_ref[hh]
        ot = acc[0:LANE] * (1.0 / acc[LANE:LANE + 1])
        odt = ot[:, :tq] - lam * ot[:, tq:]
        yt = odt * lax.rsqrt(jnp.mean(odt * odt, axis=0, keepdims=True) + EPS)
        o_ref[:, hh * LANE:(hh + 1) * LANE] = (yt.T * og_ref[...] * (1.0 - lam_init)).astype(o_ref.dtype)


def _attention(u4, lam_vec, out_gain, qk_gain, bias0, bias1, *, lam_init):
    _, bsz, seq, _ = u4.shape
    tq = bias0.shape[-1]
    nq = seq // tq
    hp = ATTN_HEADS_PER_STEP
    kv_spec = lambda base: pl.BlockSpec((hp, 1, seq, LANE), lambda b, h, i: (base // hp + h, b, 0, 0))
    bias_spec = pl.BlockSpec((hp, tq, tq), lambda b, h, i: (h, 0, 0))
    vec = pl.BlockSpec((1, LANE), lambda b, h, i: (0, 0))
    q_gain = jnp.tile(qk_gain[0] * (DH_A ** -0.5 * LOG2_E), 2).reshape(1, LANE)
    k_gain = jnp.tile(qk_gain[1], 2).reshape(1, LANE)
    return pl.pallas_call(
        functools.partial(_attn_kernel, tq=tq, lam_init=lam_init),
        grid=(bsz, H_A // hp, nq),
        in_specs=[
            pl.BlockSpec((4, DH_A), lambda b, h, i: (0, 0)),
            vec, vec, vec,
            pl.BlockSpec((hp, 1, tq, LANE), lambda b, h, i: (BLK_AQ // hp + h, b, i, 0)),
            kv_spec(BLK_AK), kv_spec(BLK_AV), bias_spec, bias_spec,
        ],
        out_specs=pl.BlockSpec((tq, hp * LANE), lambda b, h, i: (b * nq + i, h)),
        out_shape=jax.ShapeDtypeStruct((bsz * seq, MIX_WIDTH), BF16),
        scratch_shapes=[pltpu.VMEM((hp, seq, LANE), BF16),
                        pltpu.VMEM((hp, LANE + ONES_ROWS, seq), BF16),
                        pltpu.VMEM((hp, LANE, 2 * tq), BF16),
                        pltpu.VMEM((hp, tq, 2 * tq), F32),
                        pltpu.VMEM((hp, tq, 2 * tq), F32),
                        pltpu.VMEM((hp, 8, 2 * tq), F32),
                        pltpu.VMEM((hp, LANE + ONES_ROWS, 2 * tq), F32)],
        compiler_params=_params("parallel", "parallel", "arbitrary"),
        name="diff_attn",
    )(lam_vec, out_gain.reshape(1, LANE), q_gain, k_gain, u4, u4, u4, bias0, bias1)


GLA_TILE = 2 * CHUNK
HGRN_HEADS_PER_STEP = 4
GLA_HEADS_PER_STEP = 4


def _gla_block(q, k, v, g, st_ref):
    r_rows, dk = q.shape
    n8 = r_rows // SUBLANES
    n_chunks = r_rows // CHUNK
    per_chunk = CHUNK // SUBLANES
    sub = lax.broadcasted_iota(jnp.int32, (1, SUBLANES, dk), 1)
    q3, k3, g3 = (a.reshape(n8, SUBLANES, dk) for a in (q, k, g * LOG2_E))

    p = g3
    sh = 1
    while sh < SUBLANES:
        p = p + jnp.where(sub >= sh, pltpu.roll(p, sh, axis=1), 0.0)
        sh *= 2
    p4 = p.reshape(n_chunks, per_chunk, SUBLANES, dk)
    total = p4[:, 0, SUBLANES - 1:, :]
    groups = [p4[:, 0]]
    for j in range(1, per_chunk):
        groups.append(p4[:, j] + total)
        total = total + p4[:, j, SUBLANES - 1:, :]
    b4 = jnp.stack(groups, axis=1)
    b3 = b4.reshape(n8, SUBLANES, dk)

    levels = []
    w = CHUNK // 2
    while w >= SUBLANES:
        m = w // SUBLANES
        pairs = (n8 // (2 * m), 2, m, SUBLANES, dk)
        b5, q5, k5 = b3.reshape(pairs), q3.reshape(pairs), k3.reshape(pairs)
        ref = b5[:, 0:1, m - 1:, SUBLANES - 1:, :]
        d = jnp.concatenate([ref - b5[:, 0:1], b5[:, 1:2] - ref], axis=1)
        x = jnp.concatenate([k5[:, 0:1], q5[:, 1:2]], axis=1) * jnp.exp2(d)
        levels.append((w, x.reshape(r_rows, dk).astype(BF16)))
        w //= 2
    while w >= 1:
        odd = (sub & w) != 0
        if w == 4:
            ref = b3[:, 3:4, :]
            d = jnp.where(odd, b3 - ref, ref - b3)
        elif w == 2:
            ref = jnp.where(sub < 4, b3[:, 1:2, :], b3[:, 5:6, :])
            d = jnp.where(odd, b3 - ref, ref - b3)
        else:
            d = jnp.where(odd, g3, 0.0)
        x = jnp.where(odd, q3, k3) * jnp.exp2(d)
        levels.append((w, x.reshape(r_rows, dk).astype(BF16)))
        w //= 2
    qb = q.astype(BF16)
    kb = k.astype(BF16)

    ti = lax.broadcasted_iota(jnp.int32, (GLA_TILE, 1), 0)
    si = lax.broadcasted_iota(jnp.int32, (1, GLA_TILE), 1)
    lev = jnp.where(si < ti, jnp.bitwise_xor(ti, si), 0)
    masks = [(lev >= w) & (lev < 2 * w) for w, _ in levels]
    diag = ti == si

    intra = []
    for sb in range(r_rows // GLA_TILE):
        sl = slice(sb * GLA_TILE, (sb + 1) * GLA_TILE)
        sc = jnp.where(diag, _dot_nt(qb[sl], kb[sl]), 0.0)
        for (w, x), mk in zip(levels, masks):
            sc = jnp.where(mk, _dot_nt(x[sl], x[sl]), sc)
        intra.append(_dot(sc.astype(BF16), v[sl]))

    chunked = (n_chunks, per_chunk, SUBLANES, dk)
    b_last = total[:, None]
    qe = (q3.reshape(chunked) * jnp.exp2(b4)).reshape(r_rows, dk).astype(BF16)
    ke = (k3.reshape(chunked) * jnp.exp2(b_last - b4)).reshape(r_rows, dk).astype(BF16)
    dec = jnp.exp2(total)
    st = st_ref[...]
    inter = []
    for c in range(n_chunks):
        sl = slice(c * CHUNK, (c + 1) * CHUNK)
        inter.append(_dot_nt(qe[sl], st.astype(BF16)))
        st = st * dec[c] + _dot_tn(v[sl], ke[sl])
    st_ref[...] = st
    return jnp.concatenate(intra, axis=0) + jnp.concatenate(inter, axis=0)


def _hgrn_kernel(lb_ref, og_ref, q_ref, f_ref, i_ref, g_ref, o_ref, st_ref):
    @pl.when(pl.program_id(2) == 0)
    def _():
        st_ref[...] = jnp.zeros_like(st_ref)

    for r in range(0, o_ref.shape[0], GLA_TILE):
        rows = slice(r, r + GLA_TILE)
        for hh in range(HGRN_HEADS_PER_STEP):
            lb = lb_ref[hh]
            zf = f_ref[hh, 0, rows, :].astype(F32)
            t = jnp.exp(-jnp.abs(zf))
            inv = 1.0 / (1.0 + t)
            pos = zf >= 0.0
            sig = jnp.where(pos, 1.0, t) * inv
            log_f = jnp.where(lb > 0.0, jnp.log(lb + (1.0 - lb) * sig),
                              jnp.minimum(zf, 0.0) - jnp.log(1.0 + t))
            k = (1.0 - lb) * (jnp.where(pos, t, 1.0) * inv)
            o = _gla_block(q_ref[hh, 0, rows, :].astype(F32), k, i_ref[hh, 0, rows, :], log_f, st_ref.at[hh])
            o = o * _sigmoid(g_ref[hh, 0, rows, :].astype(F32))
            o_ref[rows, hh * DV_B:(hh + 1) * DV_B] = (_rms(o) * og_ref[...]).astype(o_ref.dtype)


def _hgrn(u4, lb, out_gain, tc):
    _, bsz, seq, _ = u4.shape
    nt = seq // tc
    hp = HGRN_HEADS_PER_STEP
    blk = lambda base: pl.BlockSpec((hp, 1, tc, LANE), lambda b, h, i: (base // hp + h, b, i, 0))
    return pl.pallas_call(
        _hgrn_kernel,
        grid=(bsz, H_B // hp, nt),
        in_specs=[
            pl.BlockSpec((hp, 1, DK_B), lambda b, h, i: (h, 0, 0)),
            pl.BlockSpec((1, DV_B), lambda b, h, i: (0, 0)),
            blk(BLK_BQ), blk(BLK_BF), blk(BLK_BI), blk(BLK_BG),
        ],
        out_specs=pl.BlockSpec((tc, hp * DV_B), lambda b, h, i: (b * nt + i, h)),
        out_shape=jax.ShapeDtypeStruct((bsz * seq, MIX_WIDTH), BF16),
        scratch_shapes=[pltpu.VMEM((hp, DV_B, DK_B), F32)],
        compiler_params=_params("parallel", "parallel", "arbitrary"),
        name="hgrn2",
    )(lb.reshape(H_B, 1, DK_B), out_gain.reshape(1, DV_B), u4, u4, u4, u4)


def _glac_kernel(wup_ref, bup_ref, og_ref, q_ref, k_ref, v_ref, r_ref, gd_ref, o_ref, st_ref):
    @pl.when(pl.program_id(2) == 0)
    def _():
        st_ref[...] = jnp.zeros_like(st_ref)

    for r in range(0, o_ref.shape[0], GLA_TILE):
        rows = slice(r, r + GLA_TILE)
        for hh in range(GLA_HEADS_PER_STEP):
            z = _dot(gd_ref[0, 0, rows, :], wup_ref[:, hh * DK_C:(hh + 1) * DK_C]) + bup_ref[hh]
            log_a = _log_sigmoid(z) * (1.0 / GLA_GATE_NORM)
            q = q_ref[hh, 0, rows, :].astype(F32) * (DK_C ** -0.5)
            v = jnp.concatenate([v_ref[2 * hh, 0, rows, :], v_ref[2 * hh + 1, 0, rows, :]], axis=1)
            o = _gla_block(q, k_ref[hh, 0, rows, :].astype(F32), v, log_a, st_ref.at[hh])
            gate = jnp.concatenate([r_ref[2 * hh, 0, rows, :], r_ref[2 * hh + 1, 0, rows, :]], axis=1).astype(F32)
            o_ref[rows, hh * DV_C:(hh + 1) * DV_C] = (
                _rms(o) * og_ref[...] * (gate * _sigmoid(gate))).astype(o_ref.dtype)


def _glac(u4, w_up_p, b_up, out_gain, tc):
    _, bsz, seq, _ = u4.shape
    nt = seq // tc
    hp = GLA_HEADS_PER_STEP
    one = lambda base: pl.BlockSpec((hp, 1, tc, LANE), lambda b, h, i: (base // hp + h, b, i, 0))
    two = lambda base: pl.BlockSpec((2 * hp, 1, tc, LANE), lambda b, h, i: (base // (2 * hp) + h, b, i, 0))
    return pl.pallas_call(
        _glac_kernel,
        grid=(bsz, H_C // hp, nt),
        in_specs=[
            pl.BlockSpec((LANE, hp * DK_C), lambda b, h, i: (0, h)),
            pl.BlockSpec((hp, 1, DK_C), lambda b, h, i: (h, 0, 0)),
            pl.BlockSpec((1, DV_C), lambda b, h, i: (0, 0)),
            one(BLK_CQ), one(BLK_CK), two(BLK_CV), two(BLK_CR),
            pl.BlockSpec((1, 1, tc, LANE), lambda b, h, i: (BLK_CGD, b, i, 0)),
        ],
        out_specs=pl.BlockSpec((tc, hp * DV_C), lambda b, h, i: (b * nt + i, h)),
        out_shape=jax.ShapeDtypeStruct((bsz * seq, MIX_WIDTH), BF16),
        scratch_shapes=[pltpu.VMEM((hp, DV_C, DK_C), F32)],
        compiler_params=_params("parallel", "parallel", "arbitrary"),
        name="gla",
    )(w_up_p, b_up.reshape(H_C, 1, DK_C), out_gain.reshape(1, DV_C), u4, u4, u4, u4, u4)


def _merge_kernel(ya_ref, yb_ref, yc_ref, gl_ref, wb_ref, wo_ref, x_ref, gate_ref, o_ref):
    n = pl.program_id(1)
    m_ref = o_ref

    def contribution(y_ref, branch):
        z = _dot(y_ref[...], wb_ref[branch])
        cols = []
        for c in range(D_MODEL // LANE):
            cols.append(_sigmoid(gl_ref[c].astype(F32)) * z[:, c * LANE:(c + 1) * LANE])
        return jnp.concatenate(cols, axis=1)

    @pl.when(n == 0)
    def _():
        m_ref[...] = contribution(ya_ref, 0)

    @pl.when(n == 1)
    def _():
        m_ref[...] += contribution(yb_ref, 1)

    @pl.when(n == 2)
    def _():
        m = m_ref[...] + contribution(yc_ref, 2)
        o_ref[...] = x_ref[...] + gate_ref[0] * _dot(m.astype(BF16), wo_ref[...])


def _merge(ya, yb, yc, u3, w_branch, w_out, x2, gate, seq):
    t = x2.shape[0]
    tm = min(512, seq)
    per_b = seq // tm
    ysp = pl.BlockSpec((tm, MIX_WIDTH), lambda i, n: (i, 0))
    row = pl.BlockSpec((tm, D_MODEL), lambda i, n: (i, 0))
    return pl.pallas_call(
        _merge_kernel,
        grid=(t // tm, N_BRANCH),
        in_specs=[
            ysp, ysp, ysp,
            pl.BlockSpec((D_MODEL // LANE, tm, LANE), lambda i, n: (n, i, 0)),
            pl.BlockSpec((N_BRANCH, MIX_WIDTH, D_MODEL), lambda i, n: (0, 0, 0), pipeline_mode=pl.Buffered(1)),
            pl.BlockSpec((D_MODEL, D_MODEL), lambda i, n: (0, 0), pipeline_mode=pl.Buffered(1)),
            row,
            pl.BlockSpec((1, 1, D_MODEL), lambda i, n: (i // per_b, 0, 0)),
        ],
        out_specs=row,
        out_shape=jax.ShapeDtypeStruct((t, D_MODEL), F32),
        compiler_params=_params("parallel", "arbitrary"),
        name="merge_out",
    )(ya, yb, yc, u3, w_branch, w_out, x2, gate)


def kernel(x, c, w_ada, b_ada, norm_gains, ffn_w_gate, ffn_w_up, ffn_w_down, w_in, qk_gains, diff_lambda,
           diff_out_gain, rel_bias, hgrn_lb_logits, hgrn_out_gain, gla_w_gate_up, gla_b_gate, gla_out_gain,
           w_branch, w_out):
    bsz, seq, _ = x.shape
    t = bsz * seq
    tq = min(512, seq)
    tc = min(512, seq)

    lb_all = jnp.cumsum(jax.nn.softmax(hgrn_lb_logits.astype(F32), axis=0), axis=0)
    lb_all = lb_all - lb_all[0]
    mod = _ada(c, w_ada, b_ada).reshape(DEPTH, bsz, 3, 3, 1, D_MODEL)
    bias0, bias1 = _bias_tiles(rel_bias, tq)

    x2 = x.reshape(t, D_MODEL)
    for l in range(DEPTH):
        shift, scale, gate = mod[l, :, :, 0], mod[l, :, :, 1], mod[l, :, :, 2]
        gains = norm_gains[l].reshape(4, 1, D_MODEL)
        lam_init = 0.8 - 0.6 * math.exp(-0.3 * l)

        x2 = _ffn(x2, seq, shift[:, 0], scale[:, 0], gate[:, 0], gains[0], gains[3],
                  ffn_w_gate[l, 0].astype(BF16), ffn_w_up[l, 0].astype(BF16), ffn_w_down[l, 0].astype(BF16),
                  final_norm=False)

        w_l = w_in[l]
        n_real = N_MAIN + GLA_GATE_RANK
        w_in_p = jnp.concatenate(
            [w_l[:, n_real:].astype(BF16), w_l[:, :n_real].astype(BF16),
             jnp.zeros((D_MODEL, N_IN_PAD - w_l.shape[1]), BF16)], axis=1)
        u3 = _inproj(x2, seq, shift[:, 1], scale[:, 1], gains[1], w_in_p)
        u4 = u3.reshape(N_BLK, bsz, seq, LANE)

        ya = _attention(u4, diff_lambda[l], diff_out_gain[l], qk_gains[l], bias0, bias1, lam_init=lam_init)
        yb = _hgrn(u4, lb_all[l], hgrn_out_gain[l], tc)
        w_up_p = jnp.concatenate(
            [gla_w_gate_up[l], jnp.zeros((LANE - GLA_GATE_RANK, H_C * DK_C), F32)], axis=0).astype(BF16)
        yc = _glac(u4, w_up_p, gla_b_gate[l], gla_out_gain[l], tc)

        x2 = _merge(ya, yb, yc, u3, w_branch[l].astype(BF16), w_out[l].astype(BF16), x2, gate[:, 1], seq)

        x2 = _ffn(x2, seq, shift[:, 2], scale[:, 2], gate[:, 2], gains[2], gains[3],
                  ffn_w_gate[l, 1].astype(BF16), ffn_w_up[l, 1].astype(BF16), ffn_w_down[l, 1].astype(BF16),
                  final_norm=True)
    return x2.reshape(bsz, seq, D_MODEL)
```

```python
import functools
import math

import jax
import jax.numpy as jnp
from jax import lax
from jax.experimental import pallas as pl
from jax.experimental.pallas import tpu as pltpu

D_MODEL = 2048
DEPTH = 2
CHUNK = 64
MIX_WIDTH = D_MODEL // 2
N_BRANCH = 3
H_A = 8
DH_A = 64
H_B = 8
DK_B = 128
DV_B = 128
H_C = 4
DK_C = 128
DV_C = 256
GLA_GATE_RANK = 16
GLA_GATE_NORM = 16.0
D_FF = 11 * D_MODEL // 4
N_BUCKETS = 32
MAX_DISTANCE = 128
EPS = 1e-6

LANE = 128
SUBLANES = 8
ROW_CHUNK = 256
MASK_VALUE = -1e30
LOG2_E = math.log2(math.e)
VMEM_LIMIT = 56 * 1024 * 1024

ADA_TN = 1024
FFN_TM, FFN_TF = 512, 512
IN_TM = 1024
ATTN_TQ = 512
GLA_TC = 512
MERGE_TM = 512

N_GATE_BLK = N_BRANCH * D_MODEL // LANE
BLK_AQ = N_GATE_BLK
BLK_AK = BLK_AQ + H_A
BLK_AV = BLK_AK + H_A
BLK_BQ = BLK_AV + H_A
BLK_BF = BLK_BQ + H_B
BLK_BI = BLK_BF + H_B
BLK_BG = BLK_BI + H_B
BLK_CQ = BLK_BG + H_B
BLK_CK = BLK_CQ + H_C
BLK_CV = BLK_CK + H_C
BLK_CR = BLK_CV + 2 * H_C
BLK_CGD = BLK_CR + 2 * H_C
N_MAIN = 10 * MIX_WIDTH
IN_STEP_BLOCKS = 12
N_BLK = 132
N_IN_PAD = N_BLK * LANE

BF16 = jnp.bfloat16
F32 = jnp.float32


def _params(*sem):
    return pltpu.CompilerParams(dimension_semantics=sem, vmem_limit_bytes=VMEM_LIMIT)


def _sigmoid(x):
    return 1.0 / (1.0 + jnp.exp(-x))


def _log_sigmoid(x):
    return jnp.minimum(x, 0.0) - jnp.log(1.0 + jnp.exp(-jnp.abs(x)))


def _rms(x):
    return x * lax.rsqrt(jnp.mean(x * x, axis=-1, keepdims=True) + EPS)


def _dot(a, b):
    return jnp.dot(a, b, preferred_element_type=F32)


def _dot_nt(a, b):
    return lax.dot_general(a, b, (((1,), (1,)), ((), ())), preferred_element_type=F32)


def _dot_tn(a, b):
    return lax.dot_general(a, b, (((0,), (0,)), ((), ())), preferred_element_type=F32)


def _ada_kernel(c_ref, w_ref, b_ref, o_ref):
    c = c_ref[...]
    cond = c * _sigmoid(c)
    c_hi = cond.astype(BF16)
    c_lo = (cond - c_hi.astype(F32)).astype(BF16)
    w = w_ref[0]
    w_hi = w.astype(BF16)
    w_lo = (w - w_hi.astype(F32)).astype(BF16)
    o_ref[0] = _dot(c_hi, w_hi) + (_dot(c_lo, w_hi) + _dot(c_hi, w_lo)) + b_ref[0]


def _ada(c, w_ada, b_ada):
    bsz = c.shape[0]
    n = w_ada.shape[-1]
    tn = ADA_TN
    return pl.pallas_call(
        _ada_kernel,
        grid=(DEPTH, n // tn),
        in_specs=[
            pl.BlockSpec((bsz, D_MODEL), lambda l, j: (0, 0)),
            pl.BlockSpec((1, D_MODEL, tn), lambda l, j: (l, 0, j)),
            pl.BlockSpec((1, 1, tn), lambda l, j: (l, 0, j)),
        ],
        out_specs=pl.BlockSpec((1, bsz, tn), lambda l, j: (l, 0, j)),
        out_shape=jax.ShapeDtypeStruct((DEPTH, bsz, n), F32),
        compiler_params=_params("parallel", "parallel"),
        name="ada_mod",
    )(c, w_ada, b_ada.reshape(DEPTH, 1, n))


def _modulated(x, gain, scale, shift):
    return (_rms(x) * gain) * (1.0 + scale) + shift


def _ffn_kernel(x_ref, shift_ref, scale_ref, gate_ref, gain_ref, fgain_ref, wg_ref, wu_ref, wd_ref,
                o_ref, h_ref, *, final_norm):
    j = pl.program_id(1)
    row_chunks = [slice(r, r + ROW_CHUNK) for r in range(0, x_ref.shape[0], ROW_CHUNK)]

    @pl.when(j == 0)
    def _():
        for rows in row_chunks:
            h = _modulated(x_ref[rows, :], gain_ref[...], scale_ref[0], shift_ref[0])
            h_ref[rows, :] = h.astype(BF16)
        o_ref[...] = jnp.zeros_like(o_ref)

    h = h_ref[...]
    g = _dot(h, wg_ref[...])
    u = _dot(h, wu_ref[...])
    a = (g * _sigmoid(g) * u).astype(BF16)
    o_ref[...] += _dot(a, wd_ref[...])

    @pl.when(j == pl.num_programs(1) - 1)
    def _():
        for rows in row_chunks:
            y = x_ref[rows, :] + 0.5 * gate_ref[0] * o_ref[rows, :]
            if final_norm:
                y = _rms(y) * fgain_ref[...]
            o_ref[rows, :] = y


def _ffn(x2, seq, shift, scale, gate, gain, fgain, wg, wu, wd, *, final_norm):
    t = x2.shape[0]
    tm = min(FFN_TM, seq)
    tf = FFN_TF
    per_b = seq // tm
    row = pl.BlockSpec((tm, D_MODEL), lambda i, j: (i, 0))
    vec_b = pl.BlockSpec((1, 1, D_MODEL), lambda i, j: (i // per_b, 0, 0))
    vec = pl.BlockSpec((1, D_MODEL), lambda i, j: (0, 0))
    return pl.pallas_call(
        functools.partial(_ffn_kernel, final_norm=final_norm),
        grid=(t // tm, D_FF // tf),
        in_specs=[row, vec_b, vec_b, vec_b, vec, vec,
                  pl.BlockSpec((D_MODEL, tf), lambda i, j: (0, j)),
                  pl.BlockSpec((D_MODEL, tf), lambda i, j: (0, j)),
                  pl.BlockSpec((tf, D_MODEL), lambda i, j: (j, 0))],
        out_specs=row,
        out_shape=jax.ShapeDtypeStruct((t, D_MODEL), F32),
        scratch_shapes=[pltpu.VMEM((tm, D_MODEL), BF16)],
        compiler_params=_params("parallel", "arbitrary"),
        name="ffn",
    )(x2, shift, scale, gate, gain, fgain, wg, wu, wd)


def _inproj_kernel(x_ref, shift_ref, scale_ref, gain_ref, w_ref, o_ref, h_ref):
    @pl.when(pl.program_id(1) == 0)
    def _():
        for r in range(0, x_ref.shape[0], ROW_CHUNK):
            rows = slice(r, r + ROW_CHUNK)
            h = _modulated(x_ref[rows, :], gain_ref[...], scale_ref[0], shift_ref[0])
            h_ref[rows, :] = h.astype(BF16)

    res = _dot(h_ref[...], w_ref[...])
    for s in range(IN_STEP_BLOCKS):
        o_ref[s] = res[:, s * LANE:(s + 1) * LANE].astype(o_ref.dtype)


def _inproj(x2, seq, shift, scale, gain, w_in_p):
    t = x2.shape[0]
    tm = min(IN_TM, seq)
    per_b = seq // tm
    tn = IN_STEP_BLOCKS * LANE
    return pl.pallas_call(
        _inproj_kernel,
        grid=(t // tm, N_BLK // IN_STEP_BLOCKS),
        in_specs=[
            pl.BlockSpec((tm, D_MODEL), lambda i, j: (i, 0)),
            pl.BlockSpec((1, 1, D_MODEL), lambda i, j: (i // per_b, 0, 0)),
            pl.BlockSpec((1, 1, D_MODEL), lambda i, j: (i // per_b, 0, 0)),
            pl.BlockSpec((1, D_MODEL), lambda i, j: (0, 0)),
            pl.BlockSpec((D_MODEL, tn), lambda i, j: (0, j)),
        ],
        out_specs=pl.BlockSpec((IN_STEP_BLOCKS, tm, LANE), lambda i, j: (j, i, 0)),
        out_shape=jax.ShapeDtypeStruct((N_BLK, t, LANE), BF16),
        scratch_shapes=[pltpu.VMEM((tm, D_MODEL), BF16)],
        compiler_params=_params("parallel", "arbitrary"),
        name="in_proj",
    )(x2, shift, scale, gain, w_in_p)


def _t5_bucket(rel):
    half = N_BUCKETS // 2
    max_exact = half // 2
    ret = jnp.where(rel > 0, half, 0)
    n = jnp.abs(rel)
    nf = jnp.maximum(n, 1).astype(F32)
    large = max_exact + (jnp.log(nf / max_exact) / math.log(MAX_DISTANCE / max_exact)
                         * (half - max_exact)).astype(jnp.int32)
    large = jnp.minimum(large, half - 1)
    return ret + jnp.where(n < max_exact, n, large)


def _far_bucket_is_constant(min_dist):
    half = N_BUCKETS // 2
    max_exact = half // 2
    val = math.log(min_dist / max_exact) / math.log(MAX_DISTANCE / max_exact) * (half - max_exact)
    return max_exact + val >= half


def _bias_tiles(rel_bias, tq):
    assert _far_bucket_is_constant(tq + 1)
    kpos = jnp.arange(tq)[:, None]
    qpos = jnp.arange(tq)[None, :]
    table = (rel_bias.astype(F32) - rel_bias[N_BUCKETS // 2 - 1].astype(F32)) * LOG2_E

    def lookup(bucket):
        out = jnp.zeros((H_A, tq, tq), F32)
        for n in range(N_BUCKETS):
            out = jnp.where(bucket[None] == n, table[n][:, None, None], out)
        return out

    b0 = lookup(_t5_bucket(kpos - qpos))
    b1 = lookup(_t5_bucket(kpos - tq - qpos))
    visible = (kpos // CHUNK) <= (qpos // CHUNK)
    return jnp.where(visible[None], b0, MASK_VALUE), b1


ONES_ROWS = 16
ATTN_HEADS_PER_STEP = 2
ATTN_STRIP = 256


def _qk_norm(x, gain):
    low = lax.broadcasted_iota(jnp.int32, (1, LANE), 1) < DH_A
    sq = x * x
    lo = jnp.sum(jnp.where(low, sq, 0.0), axis=-1, keepdims=True)
    hi = jnp.sum(jnp.where(low, 0.0, sq), axis=-1, keepdims=True)
    ms = jnp.where(low, lo, hi) * (1.0 / DH_A)
    return x * lax.rsqrt(ms + EPS) * gain


def _attn_kernel(lv_ref, og_ref, qg_ref, kg_ref, q_ref, k_ref, v_ref, b0_ref, b1_ref, o_ref,
                 kn_ref, vt_ref, qst_ref, sa_ref, sb_ref, m_ref, acc_ref, *, tq, lam_init):
    qi = pl.program_id(2)
    seq = k_ref.shape[2]

    heads = range(ATTN_HEADS_PER_STEP)

    @pl.when(qi == 0)
    def _():
        for hh in heads:
            for blk in range(seq // tq):
                rows = slice(blk * tq, (blk + 1) * tq)
                kn_ref[hh, rows, :] = _qk_norm(
                    k_ref[hh, 0, rows, :].astype(F32), kg_ref[...]).astype(kn_ref.dtype)
                vt_ref[hh, 0:LANE, rows] = v_ref[hh, 0, rows, :].astype(F32).T.astype(vt_ref.dtype)
            vt_ref[hh, LANE:, :] = jnp.ones((ONES_ROWS, seq), vt_ref.dtype)

    sub = lax.broadcasted_iota(jnp.int32, (LANE, 1), 0)
    for hh in heads:
        qt = _qk_norm(q_ref[hh, 0].astype(F32), qg_ref[...]).T
        qst_ref[hh] = jnp.concatenate(
            [jnp.where(sub < DH_A, qt, 0.0), jnp.where(sub < DH_A, 0.0, qt)], axis=1).astype(qst_ref.dtype)

    m_ref[...] = jnp.full_like(m_ref, MASK_VALUE)
    acc_ref[...] = jnp.zeros_like(acc_ref)

    def logits(blk, bias_ref, dst_ref):
        for hh in heads:
            s = _dot(kn_ref[hh, pl.ds(pl.multiple_of(blk * tq, tq), tq), :], qst_ref[hh])
            if bias_ref is not None:
                bias = bias_ref[hh]
                s = s + jnp.concatenate([bias, bias], axis=1)
            dst_ref[hh] = s

    def accumulate(src_ref, blk):
        for hh in heads:
            vtb = vt_ref[hh, :, pl.ds(pl.multiple_of(blk * tq, tq), tq)]
            for c in range(0, 2 * tq, ATTN_STRIP):
                cols = slice(c, c + ATTN_STRIP)
                s = src_ref[hh, :, cols]
                m_prev = m_ref[hh, 0:1, cols]
                m_new = jnp.maximum(m_prev, jnp.max(s, axis=0, keepdims=True))
                alpha = jnp.exp2(m_prev - m_new)
                p = jnp.exp2(s - m_new).astype(BF16)
                m_ref[hh, :, cols] = jnp.broadcast_to(m_new, (m_ref.shape[1], ATTN_STRIP))
                acc_ref[hh, :, cols] = alpha * acc_ref[hh, :, cols] + _dot(vtb, p)

    n_far = jnp.maximum(qi - 1, 0)
    logits(0, None, sa_ref)

    def far_pair(i, carry):
        blk = 2 * i
        logits(blk + 1, None, sb_ref)
        accumulate(sa_ref, blk)
        logits(jnp.minimum(blk + 2, n_far - 1), None, sa_ref)
        accumulate(sb_ref, blk + 1)
        return carry

    lax.fori_loop(0, n_far // 2, far_pair, 0)

    @pl.when(n_far % 2 == 1)
    def _():
        accumulate(sa_ref, n_far - 1)

    @pl.when(qi >= 1)
    def _():
        logits(qi - 1, b1_ref, sa_ref)
        logits(qi, b0_ref, sb_ref)
        accumulate(sa_ref, qi - 1)
        accumulate(sb_ref, qi)

    @pl.when(qi == 0)
    def _():
        logits(0, b0_ref, sb_ref)
        accumulate(sb_ref, 0)

    lv = lv_ref[...]
    lam = (jnp.exp(jnp.sum(lv[0:1] * lv[1:2], axis=-1, keepdims=True))
           - jnp.exp(jnp.sum(lv[2:3] * lv[3:4], axis=-1, keepdims=True)) + lam_init)
    for hh in heads:
        acc = acc_ref[hh]
        ot = acc[0:LANE] * (1.0 / acc[LANE:LANE + 1])
        odt = ot[:, :tq] - lam * ot[:, tq:]
        yt = odt * lax.rsqrt(jnp.mean(odt * odt, axis=0, keepdims=True) + EPS)
        o_ref[:, hh * LANE:(hh + 1) * LANE] = (yt.T * og_ref[...] * (1.0 - lam_init)).astype(o_ref.dtype)


def _attention(u4, lam_vec, out_gain, qk_gain, bias0, bias1, *, lam_init):
    _, bsz, seq, _ = u4.shape
    tq = bias0.shape[-1]
    nq = seq // tq
    hp = ATTN_HEADS_PER_STEP
    kv_spec = lambda base: pl.BlockSpec((hp, 1, seq, LANE), lambda b, h, i: (base // hp + h, b, 0, 0))
    bias_spec = pl.BlockSpec((hp, tq, tq), lambda b, h, i: (h, 0, 0))
    vec = pl.BlockSpec((1, LANE), lambda b, h, i: (0, 0))
    q_gain = jnp.tile(qk_gain[0] * (DH_A ** -0.5 * LOG2_E), 2).reshape(1, LANE)
    k_gain = jnp.tile(qk_gain[1], 2).reshape(1, LANE)
    return pl.pallas_call(
        functools.partial(_attn_kernel, tq=tq, lam_init=lam_init),
        grid=(bsz, H_A // hp, nq),
        in_specs=[
            pl.BlockSpec((4, DH_A), lambda b, h, i: (0, 0)),
            vec, vec, vec,
            pl.BlockSpec((hp, 1, tq, LANE), lambda b, h, i: (BLK_AQ // hp + h, b, i, 0)),
            kv_spec(BLK_AK), kv_spec(BLK_AV), bias_spec, bias_spec,
        ],
        out_specs=pl.BlockSpec((tq, hp * LANE), lambda b, h, i: (b * nq + i, h)),
        out_shape=jax.ShapeDtypeStruct((bsz * seq, MIX_WIDTH), BF16),
        scratch_shapes=[pltpu.VMEM((hp, seq, LANE), BF16),
                        pltpu.VMEM((hp, LANE + ONES_ROWS, seq), BF16),
                        pltpu.VMEM((hp, LANE, 2 * tq), BF16),
                        pltpu.VMEM((hp, tq, 2 * tq), F32),
                        pltpu.VMEM((hp, tq, 2 * tq), F32),
                        pltpu.VMEM((hp, 8, 2 * tq), F32),
                        pltpu.VMEM((hp, LANE + ONES_ROWS, 2 * tq), F32)],
        compiler_params=_params("parallel", "parallel", "arbitrary"),
        name="diff_attn",
    )(lam_vec, out_gain.reshape(1, LANE), q_gain, k_gain, u4, u4, u4, bias0, bias1)


GLA_TILE = 2 * CHUNK
HGRN_HEADS_PER_STEP = 8
GLA_HEADS_PER_STEP = 4


def _gla_block(q, k, v, g, st_ref):
    r_rows, dk = q.shape
    n8 = r_rows // SUBLANES
    n_chunks = r_rows // CHUNK
    per_chunk = CHUNK // SUBLANES
    sub = lax.broadcasted_iota(jnp.int32, (1, SUBLANES, dk), 1)
    q3, k3, g3 = (a.reshape(n8, SUBLANES, dk) for a in (q, k, g * LOG2_E))

    p = g3
    sh = 1
    while sh < SUBLANES:
        p = p + jnp.where(sub >= sh, pltpu.roll(p, sh, axis=1), 0.0)
        sh *= 2
    p4 = p.reshape(n_chunks, per_chunk, SUBLANES, dk)
    total = p4[:, 0, SUBLANES - 1:, :]
    groups = [p4[:, 0]]
    for j in range(1, per_chunk):
        groups.append(p4[:, j] + total)
        total = total + p4[:, j, SUBLANES - 1:, :]
    b4 = jnp.stack(groups, axis=1)
    b3 = b4.reshape(n8, SUBLANES, dk)

    levels = []
    w = CHUNK // 2
    while w >= SUBLANES:
        m = w // SUBLANES
        pairs = (n8 // (2 * m), 2, m, SUBLANES, dk)
        b5, q5, k5 = b3.reshape(pairs), q3.reshape(pairs), k3.reshape(pairs)
        ref = b5[:, 0:1, m - 1:, SUBLANES - 1:, :]
        d = jnp.concatenate([ref - b5[:, 0:1], b5[:, 1:2] - ref], axis=1)
        x = jnp.concatenate([k5[:, 0:1], q5[:, 1:2]], axis=1) * jnp.exp2(d)
        levels.append((w, x.reshape(r_rows, dk).astype(BF16)))
        w //= 2
    while w >= 1:
        odd = (sub & w) != 0
        if w == 4:
            ref = b3[:, 3:4, :]
            d = jnp.where(odd, b3 - ref, ref - b3)
        elif w == 2:
            ref = jnp.where(sub < 4, b3[:, 1:2, :], b3[:, 5:6, :])
            d = jnp.where(odd, b3 - ref, ref - b3)
        else:
            d = jnp.where(odd, g3, 0.0)
        x = jnp.where(odd, q3, k3) * jnp.exp2(d)
        levels.append((w, x.reshape(r_rows, dk).astype(BF16)))
        w //= 2
    qb = q.astype(BF16)
    kb = k.astype(BF16)

    ti = lax.broadcasted_iota(jnp.int32, (GLA_TILE, 1), 0)
    si = lax.broadcasted_iota(jnp.int32, (1, GLA_TILE), 1)
    lev = jnp.where(si < ti, jnp.bitwise_xor(ti, si), 0)
    masks = [(lev >= w) & (lev < 2 * w) for w, _ in levels]
    diag = ti == si

    intra = []
    for sb in range(r_rows // GLA_TILE):
        sl = slice(sb * GLA_TILE, (sb + 1) * GLA_TILE)
        sc = jnp.where(diag, _dot_nt(qb[sl], kb[sl]), 0.0)
        for (w, x), mk in zip(levels, masks):
            sc = jnp.where(mk, _dot_nt(x[sl], x[sl]), sc)
        intra.append(_dot(sc.astype(BF16), v[sl]))

    chunked = (n_chunks, per_chunk, SUBLANES, dk)
    b_last = total[:, None]
    qe = (q3.reshape(chunked) * jnp.exp2(b4)).reshape(r_rows, dk).astype(BF16)
    ke = (k3.reshape(chunked) * jnp.exp2(b_last - b4)).reshape(r_rows, dk).astype(BF16)
    dec = jnp.exp2(total)
    st = st_ref[...]
    inter = []
    for c in range(n_chunks):
        sl = slice(c * CHUNK, (c + 1) * CHUNK)
        inter.append(_dot_nt(qe[sl], st.astype(BF16)))
        st = st * dec[c] + _dot_tn(v[sl], ke[sl])
    st_ref[...] = st
    return jnp.concatenate(intra, axis=0) + jnp.concatenate(inter, axis=0)


def _hgrn_kernel(lb_ref, og_ref, q_ref, f_ref, i_ref, g_ref, o_ref, st_ref):
    @pl.when(pl.program_id(2) == 0)
    def _():
        st_ref[...] = jnp.zeros_like(st_ref)

    for r in range(0, o_ref.shape[0], GLA_TILE):
        rows = slice(r, r + GLA_TILE)
        for hh in range(HGRN_HEADS_PER_STEP):
            lb = lb_ref[hh]
            zf = f_ref[hh, 0, rows, :].astype(F32)
            t = jnp.exp(-jnp.abs(zf))
            inv = 1.0 / (1.0 + t)
            pos = zf >= 0.0
            sig = jnp.where(pos, 1.0, t) * inv
            log_f = jnp.where(lb > 0.0, jnp.log(lb + (1.0 - lb) * sig),
                              jnp.minimum(zf, 0.0) - jnp.log(1.0 + t))
            k = (1.0 - lb) * (jnp.where(pos, t, 1.0) * inv)
            o = _gla_block(q_ref[hh, 0, rows, :].astype(F32), k, i_ref[hh, 0, rows, :], log_f, st_ref.at[hh])
            o = o * _sigmoid(g_ref[hh, 0, rows, :].astype(F32))
            o_ref[rows, hh * DV_B:(hh + 1) * DV_B] = (_rms(o) * og_ref[...]).astype(o_ref.dtype)


def _hgrn(u4, lb, out_gain, tc):
    _, bsz, seq, _ = u4.shape
    nt = seq // tc
    hp = HGRN_HEADS_PER_STEP
    blk = lambda base: pl.BlockSpec((hp, 1, tc, LANE), lambda b, h, i: (base // hp + h, b, i, 0))
    return pl.pallas_call(
        _hgrn_kernel,
        grid=(bsz, H_B // hp, nt),
        in_specs=[
            pl.BlockSpec((hp, 1, DK_B), lambda b, h, i: (h, 0, 0)),
            pl.BlockSpec((1, DV_B), lambda b, h, i: (0, 0)),
            blk(BLK_BQ), blk(BLK_BF), blk(BLK_BI), blk(BLK_BG),
        ],
        out_specs=pl.BlockSpec((tc, hp * DV_B), lambda b, h, i: (b * nt + i, h)),
        out_shape=jax.ShapeDtypeStruct((bsz * seq, MIX_WIDTH), BF16),
        scratch_shapes=[pltpu.VMEM((hp, DV_B, DK_B), F32)],
        compiler_params=_params("parallel", "parallel", "arbitrary"),
        name="hgrn2",
    )(lb.reshape(H_B, 1, DK_B), out_gain.reshape(1, DV_B), u4, u4, u4, u4)


def _glac_kernel(wup_ref, bup_ref, og_ref, q_ref, k_ref, v_ref, r_ref, gd_ref, o_ref, st_ref):
    @pl.when(pl.program_id(2) == 0)
    def _():
        st_ref[...] = jnp.zeros_like(st_ref)

    for r in range(0, o_ref.shape[0], GLA_TILE):
        rows = slice(r, r + GLA_TILE)
        for hh in range(GLA_HEADS_PER_STEP):
            z = _dot(gd_ref[0, 0, rows, :], wup_ref[:, hh * DK_C:(hh + 1) * DK_C]) + bup_ref[hh]
            log_a = _log_sigmoid(z) * (1.0 / GLA_GATE_NORM)
            q = q_ref[hh, 0, rows, :].astype(F32) * (DK_C ** -0.5)
            v = jnp.concatenate([v_ref[2 * hh, 0, rows, :], v_ref[2 * hh + 1, 0, rows, :]], axis=1)
            o = _gla_block(q, k_ref[hh, 0, rows, :].astype(F32), v, log_a, st_ref.at[hh])
            gate = jnp.concatenate([r_ref[2 * hh, 0, rows, :], r_ref[2 * hh + 1, 0, rows, :]], axis=1).astype(F32)
            o_ref[rows, hh * DV_C:(hh + 1) * DV_C] = (
                _rms(o) * og_ref[...] * (gate * _sigmoid(gate))).astype(o_ref.dtype)


def _glac(u4, w_up_p, b_up, out_gain, tc):
    _, bsz, seq, _ = u4.shape
    nt = seq // tc
    hp = GLA_HEADS_PER_STEP
    one = lambda base: pl.BlockSpec((hp, 1, tc, LANE), lambda b, h, i: (base // hp + h, b, i, 0))
    two = lambda base: pl.BlockSpec((2 * hp, 1, tc, LANE), lambda b, h, i: (base // (2 * hp) + h, b, i, 0))
    return pl.pallas_call(
        _glac_kernel,
        grid=(bsz, H_C // hp, nt),
        in_specs=[
            pl.BlockSpec((LANE, hp * DK_C), lambda b, h, i: (0, h)),
            pl.BlockSpec((hp, 1, DK_C), lambda b, h, i: (h, 0, 0)),
            pl.BlockSpec((1, DV_C), lambda b, h, i: (0, 0)),
            one(BLK_CQ), one(BLK_CK), two(BLK_CV), two(BLK_CR),
            pl.BlockSpec((1, 1, tc, LANE), lambda b, h, i: (BLK_CGD, b, i, 0)),
        ],
        out_specs=pl.BlockSpec((tc, hp * DV_C), lambda b, h, i: (b * nt + i, h)),
        out_shape=jax.ShapeDtypeStruct((bsz * seq, MIX_WIDTH), BF16),
        scratch_shapes=[pltpu.VMEM((hp, DV_C, DK_C), F32)],
        compiler_params=_params("parallel", "parallel", "arbitrary"),
        name="gla",
    )(w_up_p, b_up.reshape(H_C, 1, DK_C), out_gain.reshape(1, DV_C), u4, u4, u4, u4, u4)


def _merge_kernel(ya_ref, yb_ref, yc_ref, gl_ref, wb_ref, wo_ref, x_ref, gate_ref, o_ref):
    n = pl.program_id(1)
    m_ref = o_ref

    def contribution(y_ref, branch):
        z = _dot(y_ref[...], wb_ref[branch])
        cols = []
        for c in range(D_MODEL // LANE):
            cols.append(_sigmoid(gl_ref[c].astype(F32)) * z[:, c * LANE:(c + 1) * LANE])
        return jnp.concatenate(cols, axis=1)

    @pl.when(n == 0)
    def _():
        m_ref[...] = contribution(ya_ref, 0)

    @pl.when(n == 1)
    def _():
        m_ref[...] += contribution(yb_ref, 1)

    @pl.when(n == 2)
    def _():
        m = m_ref[...] + contribution(yc_ref, 2)
        o_ref[...] = x_ref[...] + gate_ref[0] * _dot(m.astype(BF16), wo_ref[...])


def _merge(ya, yb, yc, u3, w_branch, w_out, x2, gate, seq):
    t = x2.shape[0]
    tm = min(MERGE_TM, seq)
    per_b = seq // tm
    ysp =pl.BlockSpec((tm, MIX_WIDTH), lambda i, n: (i, 0))
    row = pl.BlockSpec((tm, D_MODEL), lambda i, n: (i, 0))
    return pl.pallas_call(
        _merge_kernel,
        grid=(t // tm, N_BRANCH),
        in_specs=[
            ysp, ysp, ysp,
            pl.BlockSpec((D_MODEL // LANE, tm, LANE), lambda i, n: (n, i, 0)),
            pl.BlockSpec((N_BRANCH, MIX_WIDTH, D_MODEL), lambda i, n: (0, 0, 0), pipeline_mode=pl.Buffered(1)),
            pl.BlockSpec((D_MODEL, D_MODEL), lambda i, n: (0, 0), pipeline_mode=pl.Buffered(1)),
            row,
            pl.BlockSpec((1, 1, D_MODEL), lambda i, n: (i // per_b, 0, 0)),
        ],
        out_specs=row,
        out_shape=jax.ShapeDtypeStruct((t, D_MODEL), F32),
        compiler_params=_params("parallel", "arbitrary"),
        name="merge_out",
    )(ya, yb, yc, u3, w_branch, w_out, x2, gate)


def kernel(x, c, w_ada, b_ada, norm_gains, ffn_w_gate, ffn_w_up, ffn_w_down, w_in, qk_gains, diff_lambda,
           diff_out_gain, rel_bias, hgrn_lb_logits, hgrn_out_gain, gla_w_gate_up, gla_b_gate, gla_out_gain,
           w_branch, w_out):
    bsz, seq, _ = x.shape
    t = bsz * seq
    tq = min(ATTN_TQ, seq)
    tc = min(GLA_TC, seq)

    lb_all = jnp.cumsum(jax.nn.softmax(hgrn_lb_logits.astype(F32), axis=0), axis=0)
    lb_all = lb_all - lb_all[0]
    mod = _ada(c, w_ada, b_ada).reshape(DEPTH, bsz, 3, 3, 1, D_MODEL)
    bias0, bias1 = _bias_tiles(rel_bias, tq)

    x2 = x.reshape(t, D_MODEL)
    for l in range(DEPTH):
        shift, scale, gate = mod[l, :, :, 0], mod[l, :, :, 1], mod[l, :, :, 2]
        gains = norm_gains[l].reshape(4, 1, D_MODEL)
        lam_init = 0.8 - 0.6 * math.exp(-0.3 * l)

        x2 = _ffn(x2, seq, shift[:, 0], scale[:, 0], gate[:, 0], gains[0], gains[3],
                  ffn_w_gate[l, 0].astype(BF16), ffn_w_up[l, 0].astype(BF16), ffn_w_down[l, 0].astype(BF16),
                  final_norm=False)

        w_l = w_in[l]
        n_real = N_MAIN + GLA_GATE_RANK
        w_in_p = jnp.concatenate(
            [w_l[:, n_real:].astype(BF16), w_l[:, :n_real].astype(BF16),
             jnp.zeros((D_MODEL, N_IN_PAD - w_l.shape[1]), BF16)], axis=1)
        u3 = _inproj(x2, seq, shift[:, 1], scale[:, 1], gains[1], w_in_p)
        u4 = u3.reshape(N_BLK, bsz, seq, LANE)

        ya = _attention(u4, diff_lambda[l], diff_out_gain[l], qk_gains[l], bias0, bias1, lam_init=lam_init)
        yb = _hgrn(u4, lb_all[l], hgrn_out_gain[l], tc)
        w_up_p = jnp.concatenate(
            [gla_w_gate_up[l], jnp.zeros((LANE - GLA_GATE_RANK, H_C * DK_C), F32)], axis=0).astype(BF16)
        yc = _glac(u4, w_up_p, gla_b_gate[l], gla_out_gain[l], tc)

        x2 = _merge(ya, yb, yc, u3, w_branch[l].astype(BF16), w_out[l].astype(BF16), x2, gate[:, 1], seq)

        x2 = _ffn(x2, seq, shift[:, 2], scale[:, 2], gate[:, 2], gains[2], gains[3],
                  ffn_w_gate[l, 1].astype(BF16), ffn_w_up[l, 1].astype(BF16), ffn_w_down[l, 1].astype(BF16),
                  final_norm=True)
    return x2.reshape(bsz, seq, D_MODEL)
```

```python
import functools
import math

import jax
import jax.numpy as jnp
from jax import lax
from jax.experimental import pallas as pl
from jax.experimental.pallas import tpu as pltpu

D_MODEL = 2048
DEPTH = 2
CHUNK = 64
MIX_WIDTH = D_MODEL // 2
N_BRANCH = 3
H_A = 8
DH_A = 64
H_B = 8
DK_B = 128
DV_B = 128
H_C = 4
DK_C = 128
DV_C = 256
GLA_GATE_RANK = 16
GLA_GATE_NORM = 16.0
D_FF = 11 * D_MODEL // 4
N_BUCKETS = 32
MAX_DISTANCE = 128
EPS = 1e-6

LANE = 128
SUBLANES = 8
ROW_CHUNK = 256
MASK_VALUE = -1e30
LOG2_E = math.log2(math.e)
VMEM_LIMIT = 56 * 1024 * 1024

ADA_TN = 1024
FFN_TM, FFN_TF = 512, 512
IN_TM = 1024
ATTN_TQ = 512
GLA_TC = 512
MERGE_TM = 512

N_GATE_BLK = N_BRANCH * D_MODEL // LANE
BLK_AQ = N_GATE_BLK
BLK_AK = BLK_AQ + H_A
BLK_AV = BLK_AK + H_A
BLK_BQ = BLK_AV + H_A
BLK_BF = BLK_BQ + H_B
BLK_BI = BLK_BF + H_B
BLK_BG = BLK_BI + H_B
BLK_CQ = BLK_BG + H_B
BLK_CK = BLK_CQ + H_C
BLK_CV = BLK_CK + H_C
BLK_CR = BLK_CV + 2 * H_C
BLK_CGD = BLK_CR + 2 * H_C
N_MAIN = 10 * MIX_WIDTH
IN_STEP_BLOCKS = 12
N_BLK = 132
N_IN_PAD = N_BLK * LANE

BF16 = jnp.bfloat16
F32 = jnp.float32


def _params(*sem):
    return pltpu.CompilerParams(dimension_semantics=sem, vmem_limit_bytes=VMEM_LIMIT)


def _sigmoid(x):
    return 1.0 / (1.0 + jnp.exp(-x))


def _log_sigmoid(x):
    return jnp.minimum(x, 0.0) - jnp.log(1.0 + jnp.exp(-jnp.abs(x)))


def _rms(x):
    return x * lax.rsqrt(jnp.mean(x * x, axis=-1, keepdims=True) + EPS)


def _dot(a, b):
    return jnp.dot(a, b, preferred_element_type=F32)


def _dot_nt(a, b):
    return lax.dot_general(a, b, (((1,), (1,)), ((), ())), preferred_element_type=F32)


def _dot_tn(a, b):
    return lax.dot_general(a, b, (((0,), (0,)), ((), ())), preferred_element_type=F32)


def _ada_kernel(c_ref, w_ref, b_ref, o_ref):
    c = c_ref[...]
    cond = c * _sigmoid(c)
    c_hi = cond.astype(BF16)
    c_lo = (cond - c_hi.astype(F32)).astype(BF16)
    w = w_ref[0]
    w_hi = w.astype(BF16)
    w_lo = (w - w_hi.astype(F32)).astype(BF16)
    o_ref[0] = _dot(c_hi, w_hi) + (_dot(c_lo, w_hi) + _dot(c_hi, w_lo)) + b_ref[0]


def _ada(c, w_ada, b_ada):
    bsz = c.shape[0]
    n = w_ada.shape[-1]
    tn = ADA_TN
    return pl.pallas_call(
        _ada_kernel,
        grid=(DEPTH, n // tn),
        in_specs=[
            pl.BlockSpec((bsz, D_MODEL), lambda l, j: (0, 0)),
            pl.BlockSpec((1, D_MODEL, tn), lambda l, j: (l, 0, j)),
            pl.BlockSpec((1, 1, tn), lambda l, j: (l, 0, j)),
        ],
        out_specs=pl.BlockSpec((1, bsz, tn), lambda l, j: (l, 0, j)),
        out_shape=jax.ShapeDtypeStruct((DEPTH, bsz, n), F32),
        compiler_params=_params("parallel", "parallel"),
        name="ada_mod",
    )(c, w_ada, b_ada.reshape(DEPTH, 1, n))


def _modulated(x, gain, scale, shift):
    return (_rms(x) * gain) * (1.0 + scale) + shift


def _ffn_kernel(x_ref, shift_ref, scale_ref, gate_ref, gain_ref, fgain_ref, wg_ref, wu_ref, wd_ref,
                o_ref, h_ref, *, final_norm):
    j = pl.program_id(1)
    row_chunks = [slice(r, r + ROW_CHUNK) for r in range(0, x_ref.shape[0], ROW_CHUNK)]

    @pl.when(j == 0)
    def _():
        for rows in row_chunks:
            h = _modulated(x_ref[rows, :], gain_ref[...], scale_ref[0], shift_ref[0])
            h_ref[rows, :] = h.astype(BF16)
        o_ref[...] = jnp.zeros_like(o_ref)

    h = h_ref[...]
    g = _dot(h, wg_ref[...])
    u = _dot(h, wu_ref[...])
    a = (g * _sigmoid(g) * u).astype(BF16)
    o_ref[...] += _dot(a, wd_ref[...])

    @pl.when(j == pl.num_programs(1) - 1)
    def _():
        for rows in row_chunks:
            y = x_ref[rows, :] + 0.5 * gate_ref[0] * o_ref[rows, :]
            if final_norm:
                y = _rms(y) * fgain_ref[...]
            o_ref[rows, :] = y


def _ffn(x2, seq, shift, scale, gate, gain, fgain, wg, wu, wd, *, final_norm):
    t = x2.shape[0]
    tm = min(FFN_TM, seq)
    tf = FFN_TF
    per_b = seq // tm
    row = pl.BlockSpec((tm, D_MODEL), lambda i, j: (i, 0))
    vec_b = pl.BlockSpec((1, 1, D_MODEL), lambda i, j: (i // per_b, 0, 0))
    vec = pl.BlockSpec((1, D_MODEL), lambda i, j: (0, 0))
    return pl.pallas_call(
        functools.partial(_ffn_kernel, final_norm=final_norm),
        grid=(t // tm, D_FF // tf),
        in_specs=[row, vec_b, vec_b, vec_b, vec, vec,
                  pl.BlockSpec((D_MODEL, tf), lambda i, j: (0, j)),
                  pl.BlockSpec((D_MODEL, tf), lambda i, j: (0, j)),
                  pl.BlockSpec((tf, D_MODEL), lambda i, j: (j, 0))],
        out_specs=row,
        out_shape=jax.ShapeDtypeStruct((t, D_MODEL), F32),
        scratch_shapes=[pltpu.VMEM((tm, D_MODEL), BF16)],
        compiler_params=_params("parallel", "arbitrary"),
        name="ffn",
    )(x2, shift, scale, gate, gain, fgain, wg, wu, wd)


def _inproj_kernel(x_ref, shift_ref, scale_ref, gain_ref, w_ref, o_ref, h_ref):
    @pl.when(pl.program_id(1) == 0)
    def _():
        for r in range(0, x_ref.shape[0], ROW_CHUNK):
            rows = slice(r, r + ROW_CHUNK)
            h = _modulated(x_ref[rows, :], gain_ref[...], scale_ref[0], shift_ref[0])
            h_ref[rows, :] = h.astype(BF16)

    res = _dot(h_ref[...], w_ref[...])
    for s in range(IN_STEP_BLOCKS):
        o_ref[s] = res[:, s * LANE:(s + 1) * LANE].astype(o_ref.dtype)


def _inproj(x2, seq, shift, scale, gain, w_in_p):
    t = x2.shape[0]
    tm = min(IN_TM, seq)
    per_b = seq // tm
    tn = IN_STEP_BLOCKS * LANE
    return pl.pallas_call(
        _inproj_kernel,
        grid=(t // tm, N_BLK // IN_STEP_BLOCKS),
        in_specs=[
            pl.BlockSpec((tm, D_MODEL), lambda i, j: (i, 0)),
            pl.BlockSpec((1, 1, D_MODEL), lambda i, j: (i // per_b, 0, 0)),
            pl.BlockSpec((1, 1, D_MODEL), lambda i, j: (i // per_b, 0, 0)),
            pl.BlockSpec((1, D_MODEL), lambda i, j: (0, 0)),
            pl.BlockSpec((D_MODEL, tn), lambda i, j: (0, j)),
        ],
        out_specs=pl.BlockSpec((IN_STEP_BLOCKS, tm, LANE), lambda i, j: (j, i, 0)),
        out_shape=jax.ShapeDtypeStruct((N_BLK, t, LANE), BF16),
        scratch_shapes=[pltpu.VMEM((tm, D_MODEL), BF16)],
        compiler_params=_params("parallel", "arbitrary"),
        name="in_proj",
    )(x2, shift, scale, gain, w_in_p)


def _t5_bucket(rel):
    half = N_BUCKETS // 2
    max_exact = half // 2
    ret = jnp.where(rel > 0, half, 0)
    n = jnp.abs(rel)
    nf = jnp.maximum(n, 1).astype(F32)
    large = max_exact + (jnp.log(nf / max_exact) / math.log(MAX_DISTANCE / max_exact)
                         * (half - max_exact)).astype(jnp.int32)
    large = jnp.minimum(large, half - 1)
    return ret + jnp.where(n < max_exact, n, large)


def _far_bucket_is_constant(min_dist):
    half = N_BUCKETS // 2
    max_exact = half // 2
    val = math.log(min_dist / max_exact) / math.log(MAX_DISTANCE / max_exact) * (half - max_exact)
    return max_exact + val >= half


def _bias_tiles(rel_bias, tq):
    assert _far_bucket_is_constant(tq + 1)
    kpos = jnp.arange(tq)[:, None]
    qpos = jnp.arange(tq)[None, :]
    table = (rel_bias.astype(F32) - rel_bias[N_BUCKETS // 2 - 1].astype(F32)) * LOG2_E

    def lookup(bucket):
        out = jnp.zeros((H_A, tq, tq), F32)
        for n in range(N_BUCKETS):
            out = jnp.where(bucket[None] == n, table[n][:, None, None], out)
        return out

    b0 = lookup(_t5_bucket(kpos - qpos))
    b1 = lookup(_t5_bucket(kpos - tq - qpos))
    visible = (kpos // CHUNK) <= (qpos // CHUNK)
    return jnp.where(visible[None], b0, MASK_VALUE), b1


ONES_ROWS = 16
ATTN_HEADS_PER_STEP = 2
ATTN_STRIP = 256


def _qk_norm(x, gain):
    low = lax.broadcasted_iota(jnp.int32, (1, LANE), 1) < DH_A
    sq = x * x
    lo = jnp.sum(jnp.where(low, sq, 0.0), axis=-1, keepdims=True)
    hi = jnp.sum(jnp.where(low, 0.0, sq), axis=-1, keepdims=True)
    ms = jnp.where(low, lo, hi) * (1.0 / DH_A)
    return x * lax.rsqrt(ms + EPS) * gain


def _attn_kernel(lv_ref, og_ref, qg_ref, kg_ref, q_ref, k_ref, v_ref, b0_ref, b1_ref, o_ref,
                 kn_ref, vt_ref, qst_ref, sa_ref, sb_ref, m_ref, acc_ref, *, tq, lam_init):
    qi = pl.program_id(2)
    seq = k_ref.shape[2]

    heads = range(ATTN_HEADS_PER_STEP)

    @pl.when(qi == 0)
    def _():
        for hh in heads:
            for blk in range(seq // tq):
                rows = slice(blk * tq, (blk + 1) * tq)
                kn_ref[hh, rows, :] = _qk_norm(
                    k_ref[hh, 0, rows, :].astype(F32), kg_ref[...]).astype(kn_ref.dtype)
                vt_ref[hh, 0:LANE, rows] = v_ref[hh, 0, rows, :].astype(F32).T.astype(vt_ref.dtype)
            vt_ref[hh, LANE:, :] = jnp.ones((ONES_ROWS, seq), vt_ref.dtype)

    sub = lax.broadcasted_iota(jnp.int32, (LANE, 1), 0)
    for hh in heads:
        qt = _qk_norm(q_ref[hh, 0].astype(F32), qg_ref[...]).T
        qst_ref[hh] = jnp.concatenate(
            [jnp.where(sub < DH_A, qt, 0.0), jnp.where(sub < DH_A, 0.0, qt)], axis=1).astype(qst_ref.dtype)

    m_ref[...] = jnp.full_like(m_ref, MASK_VALUE)
    acc_ref[...] = jnp.zeros_like(acc_ref)

    def logits(blk, bias_ref, dst_ref):
        for hh in heads:
            s = _dot(kn_ref[hh, pl.ds(pl.multiple_of(blk * tq, tq), tq), :], qst_ref[hh])
            if bias_ref is not None:
                bias = bias_ref[hh]
                s = s + jnp.concatenate([bias, bias], axis=1)
            dst_ref[hh] = s

    def accumulate(src_ref, blk):
        for hh in heads:
            vtb = vt_ref[hh, :, pl.ds(pl.multiple_of(blk * tq, tq), tq)]
            for c in range(0, 2 * tq, ATTN_STRIP):
                cols = slice(c, c + ATTN_STRIP)
                s = src_ref[hh, :, cols]
                m_prev = m_ref[hh, 0:1, cols]
                m_new = jnp.maximum(m_prev, jnp.max(s, axis=0, keepdims=True))
                alpha = jnp.exp2(m_prev - m_new)
                p = jnp.exp2(s - m_new).astype(BF16)
                m_ref[hh, :, cols] = jnp.broadcast_to(m_new, (m_ref.shape[1], ATTN_STRIP))
                acc_ref[hh, :, cols] = alpha * acc_ref[hh, :, cols] + _dot(vtb, p)

    n_far = jnp.maximum(qi - 1, 0)
    logits(0, None, sa_ref)

    def far_pair(i, carry):
        blk = 2 * i
        logits(blk + 1, None, sb_ref)
        accumulate(sa_ref, blk)
        logits(jnp.minimum(blk + 2, n_far - 1), None, sa_ref)
        accumulate(sb_ref, blk + 1)
        return carry

    lax.fori_loop(0, n_far // 2, far_pair, 0)

    @pl.when(n_far % 2 == 1)
    def _():
        accumulate(sa_ref, n_far - 1)

    @pl.when(qi >= 1)
    def _():
        logits(qi - 1, b1_ref, sa_ref)
        logits(qi, b0_ref, sb_ref)
        accumulate(sa_ref, qi - 1)
        accumulate(sb_ref, qi)

    @pl.when(qi == 0)
    def _():
        logits(0, b0_ref, sb_ref)
        accumulate(sb_ref, 0)

    lv = lv_ref[...]
    lam = (jnp.exp(jnp.sum(lv[0:1] * lv[1:2], axis=-1, keepdims=True))
           - jnp.exp(jnp.sum(lv[2:3] * lv[3:4], axis=-1, keepdims=True)) + lam_init)
    for hh in heads:
        acc = acc_ref[hh]
        ot = acc[0:LANE] * (1.0 / acc[LANE:LANE + 1])
        odt = ot[:, :tq] - lam * ot[:, tq:]
        yt = odt * lax.rsqrt(jnp.mean(odt * odt, axis=0, keepdims=True) + EPS)
        o_ref[:, hh * LANE:(hh + 1) * LANE] = (yt.T * og_ref[...] * (1.0 - lam_init)).astype(o_ref.dtype)


def _attention(u4, lam_vec, out_gain, qk_gain, bias0, bias1, *, lam_init):
    _, bsz, seq, _ = u4.shape
    tq = bias0.shape[-1]
    nq = seq // tq
    hp = ATTN_HEADS_PER_STEP
    kv_spec = lambda base: pl.BlockSpec((hp, 1, seq, LANE), lambda b, h, i: (base // hp + h, b, 0, 0))
    bias_spec = pl.BlockSpec((hp, tq, tq), lambda b, h, i: (h, 0, 0))
    vec = pl.BlockSpec((1, LANE), lambda b, h, i: (0, 0))
    q_gain = jnp.tile(qk_gain[0] * (DH_A ** -0.5 * LOG2_E), 2).reshape(1, LANE)
    k_gain = jnp.tile(qk_gain[1], 2).reshape(1, LANE)
    return pl.pallas_call(
        functools.partial(_attn_kernel, tq=tq, lam_init=lam_init),
        grid=(bsz, H_A // hp, nq),
        in_specs=[
            pl.BlockSpec((4, DH_A), lambda b, h, i: (0, 0)),
            vec, vec, vec,
            pl.BlockSpec((hp, 1, tq, LANE), lambda b, h, i: (BLK_AQ // hp + h, b, i, 0)),
            kv_spec(BLK_AK), kv_spec(BLK_AV), bias_spec, bias_spec,
        ],
        out_specs=pl.BlockSpec((tq, hp * LANE), lambda b, h, i: (b * nq + i, h)),
        out_shape=jax.ShapeDtypeStruct((bsz * seq, MIX_WIDTH), BF16),
        scratch_shapes=[pltpu.VMEM((hp, seq, LANE), BF16),
                        pltpu.VMEM((hp, LANE + ONES_ROWS, seq), BF16),
                        pltpu.VMEM((hp, LANE, 2 * tq), BF16),
                        pltpu.VMEM((hp, tq, 2 * tq), F32),
                        pltpu.VMEM((hp, tq, 2 * tq), F32),
                        pltpu.VMEM((hp, 8, 2 * tq), F32),
                        pltpu.VMEM((hp, LANE + ONES_ROWS, 2 * tq), F32)],
        compiler_params=_params("parallel", "parallel", "arbitrary"),
        name="diff_attn",
    )(lam_vec, out_gain.reshape(1, LANE), q_gain, k_gain, u4, u4, u4, bias0, bias1)


GLA_TILE = 2 * CHUNK
HGRN_HEADS_PER_STEP = 8
GLA_HEADS_PER_STEP = 4


def _gla_block(q, k, v, g, st_ref):
    r_rows, dk = q.shape
    n8 = r_rows // SUBLANES
    n_chunks = r_rows // CHUNK
    per_chunk = CHUNK // SUBLANES
    sub = lax.broadcasted_iota(jnp.int32, (1, SUBLANES, dk), 1)
    q3, k3, g3 = (a.reshape(n8, SUBLANES, dk) for a in (q, k, g * LOG2_E))

    p = g3
    sh = 1
    while sh < SUBLANES:
        p = p + jnp.where(sub >= sh, pltpu.roll(p, sh, axis=1), 0.0)
        sh *= 2
    p4 = p.reshape(n_chunks, per_chunk, SUBLANES, dk)
    total = p4[:, 0, SUBLANES - 1:, :]
    groups = [p4[:, 0]]
    for j in range(1, per_chunk):
        groups.append(p4[:, j] + total)
        total = total + p4[:, j, SUBLANES - 1:, :]
    b4 = jnp.stack(groups, axis=1)
    b3 = b4.reshape(n8, SUBLANES, dk)

    levels = []
    w = CHUNK // 2
    while w >= SUBLANES:
        m = w // SUBLANES
        pairs = (n8 // (2 * m), 2, m, SUBLANES, dk)
        b5, q5, k5 = b3.reshape(pairs), q3.reshape(pairs), k3.reshape(pairs)
        ref = b5[:, 0:1, m - 1:, SUBLANES - 1:, :]
        d = jnp.concatenate([ref - b5[:, 0:1], b5[:, 1:2] - ref], axis=1)
        x = jnp.concatenate([k5[:, 0:1], q5[:, 1:2]], axis=1) * jnp.exp2(d)
        levels.append((w, x.reshape(r_rows, dk).astype(BF16)))
        w //= 2
    while w >= 1:
        odd = (sub & w) != 0
        if w == 4:
            ref = b3[:, 3:4, :]
            d = jnp.where(odd, b3 - ref, ref - b3)
        elif w == 2:
            ref = jnp.where(sub < 4, b3[:, 1:2, :], b3[:, 5:6, :])
            d = jnp.where(odd, b3 - ref, ref - b3)
        else:
            d = jnp.where(odd, g3, 0.0)
        x = jnp.where(odd, q3, k3) * jnp.exp2(d)
        levels.append((w, x.reshape(r_rows, dk).astype(BF16)))
        w //= 2
    qb = q.astype(BF16)
    kb = k.astype(BF16)

    ti = lax.broadcasted_iota(jnp.int32, (GLA_TILE, 1), 0)
    si = lax.broadcasted_iota(jnp.int32, (1, GLA_TILE), 1)
    lev = jnp.where(si < ti, jnp.bitwise_xor(ti, si), 0)
    masks = [(lev >= w) & (lev < 2 * w) for w, _ in levels]
    diag = ti == si

    intra = []
    for sb in range(r_rows // GLA_TILE):
        sl = slice(sb * GLA_TILE, (sb + 1) * GLA_TILE)
        sc = jnp.where(diag, _dot_nt(qb[sl], kb[sl]), 0.0)
        for (w, x), mk in zip(levels, masks):
            sc = jnp.where(mk, _dot_nt(x[sl], x[sl]), sc)
        intra.append(_dot(sc.astype(BF16), v[sl]))

    chunked = (n_chunks, per_chunk, SUBLANES, dk)
    b_last = total[:, None]
    qe = (q3.reshape(chunked) * jnp.exp2(b4)).reshape(r_rows, dk).astype(BF16)
    ke = (k3.reshape(chunked) * jnp.exp2(b_last - b4)).reshape(r_rows, dk).astype(BF16)
    dec = jnp.exp2(total)
    st = st_ref[...]
    inter = []
    for c in range(n_chunks):
        sl = slice(c * CHUNK, (c + 1) * CHUNK)
        inter.append(_dot_nt(qe[sl], st.astype(BF16)))
        st = st * dec[c] + _dot_tn(v[sl], ke[sl])
    st_ref[...] = st
    return jnp.concatenate(intra, axis=0) + jnp.concatenate(inter, axis=0)


def _hgrn_kernel(lb_ref, og_ref, q_ref, f_ref, i_ref, g_ref, o_ref, st_ref):
    @pl.when(pl.program_id(2) == 0)
    def _():
        st_ref[...] = jnp.zeros_like(st_ref)

    for r in range(0, o_ref.shape[0], GLA_TILE):
        rows = slice(r, r + GLA_TILE)
        for hh in range(HGRN_HEADS_PER_STEP):
            lb = lb_ref[hh]
            zf = f_ref[hh, 0, rows, :].astype(F32)
            t = jnp.exp(-jnp.abs(zf))
            inv = 1.0 / (1.0 + t)
            pos = zf >= 0.0
            sig = jnp.where(pos, 1.0, t) * inv
            log_f = jnp.where(lb > 0.0, jnp.log(lb + (1.0 - lb) * sig),
                              jnp.minimum(zf, 0.0) - jnp.log(1.0 + t))
            k = (1.0 - lb) * (jnp.where(pos, t, 1.0) * inv)
            o = _gla_block(q_ref[hh, 0, rows, :].astype(F32), k, i_ref[hh, 0, rows, :], log_f, st_ref.at[hh])
            o = o * _sigmoid(g_ref[hh, 0, rows, :].astype(F32))
            o_ref[rows, hh * DV_B:(hh + 1) * DV_B] = (_rms(o) * og_ref[...]).astype(o_ref.dtype)


def _hgrn(u4, lb, out_gain, tc):
    _, bsz, seq, _ = u4.shape
    nt = seq // tc
    hp = HGRN_HEADS_PER_STEP
    blk = lambda base: pl.BlockSpec((hp, 1, tc, LANE), lambda b, h, i: (base // hp + h, b, i, 0))
    return pl.pallas_call(
        _hgrn_kernel,
        grid=(bsz, H_B // hp, nt),
        in_specs=[
            pl.BlockSpec((hp, 1, DK_B), lambda b, h, i: (h, 0, 0)),
            pl.BlockSpec((1, DV_B), lambda b, h, i: (0, 0)),
            blk(BLK_BQ), blk(BLK_BF), blk(BLK_BI), blk(BLK_BG),
        ],
        out_specs=pl.BlockSpec((tc, hp * DV_B), lambda b, h, i: (b * nt + i, h)),
        out_shape=jax.ShapeDtypeStruct((bsz * seq, MIX_WIDTH), BF16),
        scratch_shapes=[pltpu.VMEM((hp, DV_B, DK_B), F32)],
        compiler_params=_params("parallel", "parallel", "arbitrary"),
        name="hgrn2",
    )(lb.reshape(H_B, 1, DK_B), out_gain.reshape(1, DV_B), u4, u4, u4, u4)


def _glac_kernel(wup_ref, bup_ref, og_ref, q_ref, k_ref, v_ref, r_ref, gd_ref, o_ref, st_ref):
    @pl.when(pl.program_id(2) == 0)
    def _():
        st_ref[...] = jnp.zeros_like(st_ref)

    for r in range(0, o_ref.shape[0], GLA_TILE):
        rows = slice(r, r + GLA_TILE)
        for hh in range(GLA_HEADS_PER_STEP):
            z = _dot(gd_ref[0, 0, rows, :], wup_ref[:, hh * DK_C:(hh + 1) * DK_C]) + bup_ref[hh]
            log_a = _log_sigmoid(z) * (1.0 / GLA_GATE_NORM)
            q = q_ref[hh, 0, rows, :].astype(F32) * (DK_C ** -0.5)
            v = jnp.concatenate([v_ref[2 * hh, 0, rows, :], v_ref[2 * hh + 1, 0, rows, :]], axis=1)
            o = _gla_block(q, k_ref[hh, 0, rows, :].astype(F32), v, log_a, st_ref.at[hh])
            gate = jnp.concatenate([r_ref[2 * hh, 0, rows, :], r_ref[2 * hh + 1, 0, rows, :]], axis=1).astype(F32)
            o_ref[rows, hh * DV_C:(hh + 1) * DV_C] = (
                _rms(o) * og_ref[...] * (gate * _sigmoid(gate))).astype(o_ref.dtype)


def _glac(u4, w_up_p, b_up, out_gain, tc):
    _, bsz, seq, _ = u4.shape
    nt = seq // tc
    hp = GLA_HEADS_PER_STEP
    one = lambda base: pl.BlockSpec((hp, 1, tc, LANE), lambda b, h, i: (base // hp + h, b, i, 0))
    two = lambda base: pl.BlockSpec((2 * hp, 1, tc, LANE), lambda b, h, i: (base // (2 * hp) + h, b, i, 0))
    return pl.pallas_call(
        _glac_kernel,
        grid=(bsz, H_C // hp, nt),
        in_specs=[
            pl.BlockSpec((LANE, hp * DK_C), lambda b, h, i: (0, h)),
            pl.BlockSpec((hp, 1, DK_C), lambda b, h, i: (h, 0, 0)),
            pl.BlockSpec((1, DV_C), lambda b, h, i: (0, 0)),
            one(BLK_CQ), one(BLK_CK), two(BLK_CV), two(BLK_CR),
            pl.BlockSpec((1, 1, tc, LANE), lambda b, h, i: (BLK_CGD, b, i, 0)),
        ],
        out_specs=pl.BlockSpec((tc, hp * DV_C), lambda b, h, i: (b * nt + i, h)),
        out_shape=jax.ShapeDtypeStruct((bsz * seq, MIX_WIDTH), BF16),
        scratch_shapes=[pltpu.VMEM((hp, DV_C, DK_C), F32)],
        compiler_params=_params("parallel", "parallel", "arbitrary"),
        name="gla",
    )(w_up_p, b_up.reshape(H_C, 1, DK_C), out_gain.reshape(1, DV_C), u4, u4, u4, u4, u4)


def _merge_kernel(ya_ref, yb_ref, yc_ref, gl_ref, wb_ref, wo_ref, x_ref, gate_ref, o_ref):
    n = pl.program_id(1)
    m_ref = o_ref

    def contribution(y_ref, branch):
        z = _dot(y_ref[...], wb_ref[branch])
        cols = []
        for c in range(D_MODEL // LANE):
            cols.append(_sigmoid(gl_ref[c].astype(F32)) * z[:, c * LANE:(c + 1) * LANE])
        return jnp.concatenate(cols, axis=1)

    @pl.when(n == 0)
    def _():
        m_ref[...] = contribution(ya_ref, 0)

    @pl.when(n == 1)
    def _():
        m_ref[...] += contribution(yb_ref, 1)

    @pl.when(n == 2)
    def _():
        m = m_ref[...] + contribution(yc_ref, 2)
        o_ref[...] = x_ref[...] + gate_ref[0] * _dot(m.astype(BF16), wo_ref[...])


def _merge(ya, yb, yc, u3, w_branch, w_out, x2, gate, seq):
    t = x2.shape[0]
    tm = min(MERGE_TM, seq)
    per_b = seq // tm
    ysp = pl.BlockSpec((tm, MIX_WIDTH), lambda i, n: (i, 0))
    row = pl.BlockSpec((tm, D_MODEL), lambda i, n: (i, 0))
    return pl.pallas_call(
        _merge_kernel,
        grid=(t // tm, N_BRANCH),
        in_specs=[
            ysp, ysp, ysp,
            pl.BlockSpec((D_MODEL // LANE, tm, LANE), lambda i, n: (n, i, 0)),
            pl.BlockSpec((N_BRANCH, MIX_WIDTH, D_MODEL), lambda i, n: (0, 0, 0), pipeline_mode=pl.Buffered(1)),
            pl.BlockSpec((D_MODEL, D_MODEL), lambda i, n: (0, 0), pipeline_mode=pl.Buffered(1)),
            row,
            pl.BlockSpec((1, 1, D_MODEL), lambda i, n: (i // per_b, 0, 0)),
        ],
        out_specs=row,
        out_shape=jax.ShapeDtypeStruct((t, D_MODEL), F32),
        compiler_params=_params("parallel", "arbitrary"),
        name="merge_out",
    )(ya, yb, yc, u3, w_branch, w_out, x2, gate)


def kernel(x, c, w_ada, b_ada, norm_gains, ffn_w_gate, ffn_w_up, ffn_w_down, w_in, qk_gains, diff_lambda,
           diff_out_gain, rel_bias, hgrn_lb_logits, hgrn_out_gain, gla_w_gate_up, gla_b_gate, gla_out_gain,
           w_branch, w_out):
    bsz, seq, _ = x.shape
    t = bsz * seq
    tq = min(ATTN_TQ, seq)
    tc = min(GLA_TC, seq)

    lb_all = jnp.cumsum(jax.nn.softmax(hgrn_lb_logits.astype(F32), axis=0), axis=0)
    lb_all = lb_all - lb_all[0]
    mod = _ada(c, w_ada, b_ada).reshape(DEPTH, bsz, 3, 3, 1, D_MODEL)
    bias0, bias1 = _bias_tiles(rel_bias, tq)

    x2 = x.reshape(t, D_MODEL)
    for l in range(DEPTH):
        shift, scale, gate = mod[l, :, :, 0], mod[l, :, :, 1], mod[l, :, :, 2]
        gains = norm_gains[l].reshape(4, 1, D_MODEL)
        lam_init = 0.8 - 0.6 * math.exp(-0.3 * l)

        x2 = _ffn(x2, seq, shift[:, 0], scale[:, 0], gate[:, 0], gains[0], gains[3],
                  ffn_w_gate[l, 0].astype(BF16), ffn_w_up[l, 0].astype(BF16), ffn_w_down[l, 0].astype(BF16),
                  final_norm=False)

        w_l = w_in[l]
        n_real = N_MAIN + GLA_GATE_RANK
        w_in_p = jnp.concatenate(
            [w_l[:, n_real:].astype(BF16), w_l[:, :n_real].astype(BF16),
             jnp.zeros((D_MODEL, N_IN_PAD - w_l.shape[1]), BF16)], axis=1)
        u3 = _inproj(x2, seq, shift[:, 1], scale[:, 1], gains[1], w_in_p)
        u4 = u3.reshape(N_BLK, bsz, seq, LANE)

        ya = _attention(u4, diff_lambda[l], diff_out_gain[l], qk_gains[l], bias0, bias1, lam_init=lam_init)
        yb = _hgrn(u4, lb_all[l], hgrn_out_gain[l], tc)
        w_up_p = jnp.concatenate(
            [gla_w_gate_up[l], jnp.zeros((LANE - GLA_GATE_RANK, H_C * DK_C), F32)], axis=0).astype(BF16)
        yc = _glac(u4, w_up_p, gla_b_gate[l], gla_out_gain[l], tc)

        x2 = _merge(ya, yb, yc, u3, w_branch[l].astype(BF16), w_out[l].astype(BF16), x2, gate[:, 1], seq)

        x2 = _ffn(x2, seq, shift[:, 2], scale[:, 2], gate[:, 2], gains[2], gains[3],
                  ffn_w_gate[l, 1].astype(BF16), ffn_w_up[l, 1].astype(BF16), ffn_w_down[l, 1].astype(BF16),
                  final_norm=True)
    return x2.reshape(bsz, seq, D_MODEL)
```

```python
import functools
import math

import jax
import jax.numpy as jnp
from jax import lax
from jax.experimental import pallas as pl
from jax.experimental.pallas import tpu as pltpu

D_MODEL = 2048
DEPTH = 2
CHUNK = 64
MIX_WIDTH = D_MODEL // 2
N_BRANCH = 3
H_A = 8
DH_A = 64
H_B = 8
DK_B = 128
DV_B = 128
H_C = 4
DK_C = 128
DV_C = 256
GLA_GATE_RANK = 16
GLA_GATE_NORM = 16.0
D_FF = 11 * D_MODEL // 4
N_BUCKETS = 32
MAX_DISTANCE = 128
EPS = 1e-6

LANE = 128
SUBLANES = 8
ROW_CHUNK = 256
MASK_VALUE = -1e30
LOG2_E = math.log2(math.e)
VMEM_LIMIT = 56 * 1024 * 1024

ADA_TN = 1024
ADA_TK = 256
FFN_TM, FFN_TF = 512, 512
IN_TM = 1024
ATTN_TQ = 512
GLA_TC = 512
MERGE_TM = 512

N_GATE_BLK = N_BRANCH * D_MODEL // LANE
BLK_AQ = N_GATE_BLK
BLK_AK = BLK_AQ + H_A
BLK_AV = BLK_AK + H_A
BLK_BQ = BLK_AV + H_A
BLK_BF = BLK_BQ + H_B
BLK_BI = BLK_BF + H_B
BLK_BG = BLK_BI + H_B
BLK_CQ = BLK_BG + H_B
BLK_CK = BLK_CQ + H_C
BLK_CV = BLK_CK + H_C
BLK_CR = BLK_CV + 2 * H_C
BLK_CGD = BLK_CR + 2 * H_C
N_MAIN = 10 * MIX_WIDTH
IN_STEP_BLOCKS = 12
N_BLK = 132
N_IN_PAD = N_BLK * LANE

BF16 = jnp.bfloat16
F32 = jnp.float32


def _params(*sem):
    return pltpu.CompilerParams(dimension_semantics=sem, vmem_limit_bytes=VMEM_LIMIT)


def _sigmoid(x):
    return 1.0 / (1.0 + jnp.exp(-x))


def _log_sigmoid(x):
    return jnp.minimum(x, 0.0) - jnp.log(1.0 + jnp.exp(-jnp.abs(x)))


def _rms(x):
    return x * lax.rsqrt(jnp.mean(x * x, axis=-1, keepdims=True) + EPS)


def _dot(a, b):
    return jnp.dot(a, b, preferred_element_type=F32)


def _dot_nt(a, b):
    return lax.dot_general(a, b, (((1,), (1,)), ((), ())), preferred_element_type=F32)


def _dot_tn(a, b):
    return lax.dot_general(a, b, (((0,), (0,)), ((), ())), preferred_element_type=F32)


def _ada_kernel(c_ref, w_ref, b_ref, o_ref):
    @pl.when(pl.program_id(1) == 0)
    def _():
        o_ref[0] = jnp.broadcast_to(b_ref[0], o_ref.shape[1:])

    c = c_ref[...]
    cond = c * _sigmoid(c)
    c_hi = cond.astype(BF16)
    c_lo = (cond - c_hi.astype(F32)).astype(BF16)
    for n0 in range(0, o_ref.shape[2], ADA_TN):
        w = w_ref[0, :, n0:n0 + ADA_TN]
        w_hi = w.astype(BF16)
        w_lo = (w - w_hi.astype(F32)).astype(BF16)
        o_ref[0, :, n0:n0 + ADA_TN] += _dot(c_hi, w_hi) + (_dot(c_lo, w_hi) + _dot(c_hi, w_lo))


def _ada(c, w_ada, b_ada):
    bsz = c.shape[0]
    n = w_ada.shape[-1]
    return pl.pallas_call(
        _ada_kernel,
        grid=(DEPTH, D_MODEL // ADA_TK),
        in_specs=[
            pl.BlockSpec((bsz, ADA_TK), lambda l, k: (0, k)),
            pl.BlockSpec((1, ADA_TK, n), lambda l, k: (l, k, 0)),
            pl.BlockSpec((1, 1, n), lambda l, k: (l, 0, 0)),
        ],
        out_specs=pl.BlockSpec((1, bsz, n), lambda l, k: (l, 0, 0)),
        out_shape=jax.ShapeDtypeStruct((DEPTH, bsz, n), F32),
        compiler_params=_params("parallel", "arbitrary"),
        name="ada_mod",
    )(c, w_ada, b_ada.reshape(DEPTH, 1, n))


def _modulated(x, gain, scale, shift):
    return (_rms(x) * gain) * (1.0 + scale) + shift


def _ffn_kernel(x_ref, shift_ref, scale_ref, gate_ref, gain_ref, fgain_ref, wg_ref, wu_ref, wd_ref,
                o_ref, h_ref, *, final_norm):
    j = pl.program_id(1)
    row_chunks = [slice(r, r + ROW_CHUNK) for r in range(0, x_ref.shape[0], ROW_CHUNK)]

    @pl.when(j == 0)
    def _():
        for rows in row_chunks:
            h = _modulated(x_ref[rows, :], gain_ref[...], scale_ref[0], shift_ref[0])
            h_ref[rows, :] = h.astype(BF16)
        o_ref[...] = jnp.zeros_like(o_ref)

    h = h_ref[...]
    g = _dot(h, wg_ref[...])
    u = _dot(h, wu_ref[...])
    a = (g * _sigmoid(g) * u).astype(BF16)
    o_ref[...] += _dot(a, wd_ref[...])

    @pl.when(j == pl.num_programs(1) - 1)
    def _():
        for rows in row_chunks:
            y = x_ref[rows, :] + 0.5 * gate_ref[0] * o_ref[rows, :]
            if final_norm:
                y = _rms(y) * fgain_ref[...]
            o_ref[rows, :] = y


def _ffn(x2, seq, shift, scale, gate, gain, fgain, wg, wu, wd, *, final_norm):
    t = x2.shape[0]
    tm = min(FFN_TM, seq)
    tf = FFN_TF
    per_b = seq // tm
    row = pl.BlockSpec((tm, D_MODEL), lambda i, j: (i, 0))
    vec_b = pl.BlockSpec((1, 1, D_MODEL), lambda i, j: (i // per_b, 0, 0))
    vec = pl.BlockSpec((1, D_MODEL), lambda i, j: (0, 0))
    return pl.pallas_call(
        functools.partial(_ffn_kernel, final_norm=final_norm),
        grid=(t // tm, D_FF // tf),
        in_specs=[row, vec_b, vec_b, vec_b, vec, vec,
                  pl.BlockSpec((D_MODEL, tf), lambda i, j: (0, j)),
                  pl.BlockSpec((D_MODEL, tf), lambda i, j: (0, j)),
                  pl.BlockSpec((tf, D_MODEL), lambda i, j: (j, 0))],
        out_specs=row,
        out_shape=jax.ShapeDtypeStruct((t, D_MODEL), F32),
        scratch_shapes=[pltpu.VMEM((tm, D_MODEL), BF16)],
        compiler_params=_params("parallel", "arbitrary"),
        name="ffn",
    )(x2, shift, scale, gate, gain, fgain, wg, wu, wd)


def _inproj_kernel(x_ref, shift_ref, scale_ref, gain_ref, w_ref, o_ref, h_ref):
    @pl.when(pl.program_id(1) == 0)
    def _():
        for r in range(0, x_ref.shape[0], ROW_CHUNK):
            rows = slice(r, r + ROW_CHUNK)
            h = _modulated(x_ref[rows, :], gain_ref[...], scale_ref[0], shift_ref[0])
            h_ref[rows, :] = h.astype(BF16)

    res = _dot(h_ref[...], w_ref[...])
    for s in range(IN_STEP_BLOCKS):
        o_ref[s] = res[:, s * LANE:(s + 1) * LANE].astype(o_ref.dtype)


def _inproj(x2, seq, shift, scale, gain, w_in_p):
    t = x2.shape[0]
    tm = min(IN_TM, seq)
    per_b = seq // tm
    tn = IN_STEP_BLOCKS * LANE
    return pl.pallas_call(
        _inproj_kernel,
        grid=(t // tm, N_BLK // IN_STEP_BLOCKS),
        in_specs=[
            pl.BlockSpec((tm, D_MODEL), lambda i, j: (i, 0)),
            pl.BlockSpec((1, 1, D_MODEL), lambda i, j: (i // per_b, 0, 0)),
            pl.BlockSpec((1, 1, D_MODEL), lambda i, j: (i // per_b, 0, 0)),
            pl.BlockSpec((1, D_MODEL), lambda i, j: (0, 0)),
            pl.BlockSpec((D_MODEL, tn), lambda i, j: (0, j)),
        ],
        out_specs=pl.BlockSpec((IN_STEP_BLOCKS, tm, LANE), lambda i, j: (j, i, 0)),
        out_shape=jax.ShapeDtypeStruct((N_BLK, t, LANE), BF16),
        scratch_shapes=[pltpu.VMEM((tm, D_MODEL), BF16)],
        compiler_params=_params("parallel", "arbitrary"),
        name="in_proj",
    )(x2, shift, scale, gain, w_in_p)


def _t5_bucket(rel):
    half = N_BUCKETS // 2
    max_exact = half // 2
    ret = jnp.where(rel > 0, half, 0)
    n = jnp.abs(rel)
    nf = jnp.maximum(n, 1).astype(F32)
    large = max_exact + (jnp.log(nf / max_exact) / math.log(MAX_DISTANCE / max_exact)
                         * (half - max_exact)).astype(jnp.int32)
    large = jnp.minimum(large, half - 1)
    return ret + jnp.where(n < max_exact, n, large)


def _far_bucket_is_constant(min_dist):
    half = N_BUCKETS // 2
    max_exact = half // 2
    val = math.log(min_dist / max_exact) / math.log(MAX_DISTANCE / max_exact) * (half - max_exact)
    return max_exact + val >= half


def _bias_tiles(rel_bias, tq):
    assert _far_bucket_is_constant(tq + 1)
    kpos = jnp.arange(tq)[:, None]
    qpos = jnp.arange(tq)[None, :]
    table = (rel_bias.astype(F32) - rel_bias[N_BUCKETS // 2 - 1].astype(F32)) * LOG2_E

    def lookup(bucket):
        out = jnp.zeros((H_A, tq, tq), F32)
        for n in range(N_BUCKETS):
            out = jnp.where(bucket[None] == n, table[n][:, None, None], out)
        return out

    b0 = lookup(_t5_bucket(kpos - qpos))
    b1 = lookup(_t5_bucket(kpos - tq - qpos))
    visible = (kpos // CHUNK) <= (qpos // CHUNK)
    return jnp.where(visible[None], b0, MASK_VALUE), b1


ONES_ROWS = 16
ATTN_HEADS_PER_STEP = 2
ATTN_STRIP = 256


def _qk_norm(x, gain):
    low = lax.broadcasted_iota(jnp.int32, (1, LANE), 1) < DH_A
    sq = x * x
    lo = jnp.sum(jnp.where(low, sq, 0.0), axis=-1, keepdims=True)
    hi = jnp.sum(jnp.where(low, 0.0, sq), axis=-1, keepdims=True)
    ms = jnp.where(low, lo, hi) * (1.0 / DH_A)
    return x * lax.rsqrt(ms + EPS) * gain


def _attn_kernel(lv_ref, og_ref, qg_ref, kg_ref, q_ref, k_ref, v_ref, b0_ref, b1_ref, o_ref,
                 kn_ref, vt_ref, qst_ref, sa_ref, sb_ref, m_ref, acc_ref, *, tq, lam_init):
    qi = pl.program_id(2)
    seq = k_ref.shape[2]

    heads = range(ATTN_HEADS_PER_STEP)

    @pl.when(qi == 0)
    def _():
        for hh in heads:
            for blk in range(seq // tq):
                rows = slice(blk * tq, (blk + 1) * tq)
                kn_ref[hh, rows, :] = _qk_norm(
                    k_ref[hh, 0, rows, :].astype(F32), kg_ref[...]).astype(kn_ref.dtype)
                vt_ref[hh, 0:LANE, rows] = v_ref[hh, 0, rows, :].astype(F32).T.astype(vt_ref.dtype)
            vt_ref[hh, LANE:, :] = jnp.ones((ONES_ROWS, seq), vt_ref.dtype)

    sub = lax.broadcasted_iota(jnp.int32, (LANE, 1), 0)
    for hh in heads:
        qt = _qk_norm(q_ref[hh, 0].astype(F32), qg_ref[...]).T
        qst_ref[hh] = jnp.concatenate(
            [jnp.where(sub < DH_A, qt, 0.0), jnp.where(sub < DH_A, 0.0, qt)], axis=1).astype(qst_ref.dtype)

    m_ref[...] = jnp.full_like(m_ref, MASK_VALUE)
    acc_ref[...] = jnp.zeros_like(acc_ref)

    def logits(blk, bias_ref, dst_ref):
        for hh in heads:
            s = _dot(kn_ref[hh, pl.ds(pl.multiple_of(blk * tq, tq), tq), :], qst_ref[hh])
            if bias_ref is not None:
                bias = bias_ref[hh]
                s = s + jnp.concatenate([bias, bias], axis=1)
            dst_ref[hh] = s

    def accumulate(src_ref, blk):
        for hh in heads:
            vtb = vt_ref[hh, :, pl.ds(pl.multiple_of(blk * tq, tq), tq)]
            for c in range(0, 2 * tq, ATTN_STRIP):
                cols = slice(c, c + ATTN_STRIP)
                s = src_ref[hh, :, cols]
                m_prev = m_ref[hh, 0:1, cols]
                m_new = jnp.maximum(m_prev, jnp.max(s, axis=0, keepdims=True))
                alpha = jnp.exp2(m_prev - m_new)
                p = jnp.exp2(s - m_new).astype(BF16)
                m_ref[hh, :, cols] = jnp.broadcast_to(m_new, (m_ref.shape[1], ATTN_STRIP))
                acc_ref[hh, :, cols] = alpha * acc_ref[hh, :, cols] + _dot(vtb, p)

    n_far = jnp.maximum(qi - 1, 0)
    logits(0, None, sa_ref)

    def far_pair(i, carry):
        blk = 2 * i
        logits(blk + 1, None, sb_ref)
        accumulate(sa_ref, blk)
        logits(jnp.minimum(blk + 2, n_far - 1), None, sa_ref)
        accumulate(sb_ref, blk + 1)
        return carry

    lax.fori_loop(0, n_far // 2, far_pair, 0)

    @pl.when(n_far % 2 == 1)
    def _():
        accumulate(sa_ref, n_far - 1)

    @pl.when(qi >= 1)
    def _():
        logits(qi - 1, b1_ref, sa_ref)
        logits(qi, b0_ref, sb_ref)
        accumulate(sa_ref, qi - 1)
        accumulate(sb_ref, qi)

    @pl.when(qi == 0)
    def _():
        logits(0, b0_ref, sb_ref)
        accumulate(sb_ref, 0)

    lv = lv_ref[...]
    lam = (jnp.exp(jnp.sum(lv[0:1] * lv[1:2], axis=-1, keepdims=True))
           - jnp.exp(jnp.sum(lv[2:3] * lv[3:4], axis=-1, keepdims=True)) + lam_init)
    for hh in heads:
        acc = acc_ref[hh]
        ot = acc[0:LANE] * (1.0 / acc[LANE:LANE + 1])
        odt = ot[:, :tq] - lam * ot[:, tq:]
        yt = odt * lax.rsqrt(jnp.mean(odt * odt, axis=0, keepdims=True) + EPS)
        o_ref[:, hh * LANE:(hh + 1) * LANE] = (yt.T * og_ref[...] * (1.0 - lam_init)).astype(o_ref.dtype)


def _attention(u4, lam_vec, out_gain, qk_gain, bias0, bias1, *, lam_init):
    _, bsz, seq, _ = u4.shape
    tq = bias0.shape[-1]
    nq = seq // tq
    hp = ATTN_HEADS_PER_STEP
    kv_spec = lambda base: pl.BlockSpec((hp, 1, seq, LANE), lambda b, h, i: (base // hp + h, b, 0, 0))
    bias_spec = pl.BlockSpec((hp, tq, tq), lambda b, h, i: (h, 0, 0))
    vec = pl.BlockSpec((1, LANE), lambda b, h, i: (0, 0))
    q_gain = jnp.tile(qk_gain[0] * (DH_A ** -0.5 * LOG2_E), 2).reshape(1, LANE)
    k_gain = jnp.tile(qk_gain[1], 2).reshape(1, LANE)
    return pl.pallas_call(
        functools.partial(_attn_kernel, tq=tq, lam_init=lam_init),
        grid=(bsz, H_A // hp, nq),
        in_specs=[
            pl.BlockSpec((4, DH_A), lambda b, h, i: (0, 0)),
            vec, vec, vec,
            pl.BlockSpec((hp, 1, tq, LANE), lambda b, h, i: (BLK_AQ // hp + h, b, i, 0)),
            kv_spec(BLK_AK), kv_spec(BLK_AV), bias_spec, bias_spec,
        ],
        out_specs=pl.BlockSpec((tq, hp * LANE), lambda b, h, i: (b * nq + i, h)),
        out_shape=jax.ShapeDtypeStruct((bsz * seq, MIX_WIDTH), BF16),
        scratch_shapes=[pltpu.VMEM((hp, seq, LANE), BF16),
                        pltpu.VMEM((hp, LANE + ONES_ROWS, seq), BF16),
                        pltpu.VMEM((hp, LANE, 2 * tq), BF16),
                        pltpu.VMEM((hp, tq, 2 * tq), F32),
                        pltpu.VMEM((hp, tq, 2 * tq), F32),
                        pltpu.VMEM((hp, 8, 2 * tq), F32),
                        pltpu.VMEM((hp, LANE + ONES_ROWS, 2 * tq), F32)],
        compiler_params=_params("parallel", "parallel", "arbitrary"),
        name="diff_attn",
    )(lam_vec, out_gain.reshape(1, LANE), q_gain, k_gain, u4, u4, u4, bias0, bias1)


GLA_TILE = 2 * CHUNK
HGRN_HEADS_PER_STEP = 8
GLA_HEADS_PER_STEP = 4


def _gla_block(q, k, v, g, st_ref):
    r_rows, dk = q.shape
    n8 = r_rows // SUBLANES
    n_chunks = r_rows // CHUNK
    per_chunk = CHUNK // SUBLANES
    sub = lax.broadcasted_iota(jnp.int32, (1, SUBLANES, dk), 1)
    q3, k3, g3 = (a.reshape(n8, SUBLANES, dk) for a in (q, k, g * LOG2_E))

    p = g3
    sh = 1
    while sh < SUBLANES:
        p = p + jnp.where(sub >= sh, pltpu.roll(p, sh, axis=1), 0.0)
        sh *= 2
    p4 = p.reshape(n_chunks, per_chunk, SUBLANES, dk)
    total = p4[:, 0, SUBLANES - 1:, :]
    groups = [p4[:, 0]]
    for j in range(1, per_chunk):
        groups.append(p4[:, j] + total)
        total = total + p4[:, j, SUBLANES - 1:, :]
    b4 = jnp.stack(groups, axis=1)
    b3 = b4.reshape(n8, SUBLANES, dk)

    levels = []
    w = CHUNK // 2
    while w >= SUBLANES:
        m = w // SUBLANES
        pairs = (n8 // (2 * m), 2, m, SUBLANES, dk)
        b5, q5, k5 = b3.reshape(pairs), q3.reshape(pairs), k3.reshape(pairs)
        ref = b5[:, 0:1, m - 1:, SUBLANES - 1:, :]
        d = jnp.concatenate([ref - b5[:, 0:1], b5[:, 1:2] - ref], axis=1)
        x = jnp.concatenate([k5[:, 0:1], q5[:, 1:2]], axis=1) * jnp.exp2(d)
        levels.append((w, x.reshape(r_rows, dk).astype(BF16)))
        w //= 2
    while w >= 1:
        odd = (sub & w) != 0
        if w == 4:
            ref = b3[:, 3:4, :]
            d = jnp.where(odd, b3 - ref, ref - b3)
        elif w == 2:
            ref = jnp.where(sub < 4, b3[:, 1:2, :], b3[:, 5:6, :])
            d = jnp.where(odd, b3 - ref, ref - b3)
        else:
            d = jnp.where(odd, g3, 0.0)
        x = jnp.where(odd, q3, k3) * jnp.exp2(d)
        levels.append((w, x.reshape(r_rows, dk).astype(BF16)))
        w //= 2
    qb = q.astype(BF16)
    kb = k.astype(BF16)

    ti = lax.broadcasted_iota(jnp.int32, (GLA_TILE, 1), 0)
    si = lax.broadcasted_iota(jnp.int32, (1, GLA_TILE), 1)
    lev = jnp.where(si < ti, jnp.bitwise_xor(ti, si), 0)
    masks = [(lev >= w) & (lev < 2 * w) for w, _ in levels]
    diag = ti == si

    intra = []
    for sb in range(r_rows // GLA_TILE):
        sl = slice(sb * GLA_TILE, (sb + 1) * GLA_TILE)
        sc = jnp.where(diag, _dot_nt(qb[sl], kb[sl]), 0.0)
        for (w, x), mk in zip(levels, masks):
            sc = jnp.where(mk, _dot_nt(x[sl], x[sl]), sc)
        intra.append(_dot(sc.astype(BF16), v[sl]))

    chunked = (n_chunks, per_chunk, SUBLANES, dk)
    b_last = total[:, None]
    qe = (q3.reshape(chunked) * jnp.exp2(b4)).reshape(r_rows, dk).astype(BF16)
    ke = (k3.reshape(chunked) * jnp.exp2(b_last - b4)).reshape(r_rows, dk).astype(BF16)
    dec = jnp.exp2(total)
    st = st_ref[...]
    inter = []
    for c in range(n_chunks):
        sl = slice(c * CHUNK, (c + 1) * CHUNK)
        inter.append(_dot_nt(qe[sl], st.astype(BF16)))
        st = st * dec[c] + _dot_tn(v[sl], ke[sl])
    st_ref[...] = st
    return jnp.concatenate(intra, axis=0) + jnp.concatenate(inter, axis=0)


def _hgrn_kernel(lb_ref, og_ref, q_ref, f_ref, i_ref, g_ref, o_ref, st_ref):
    @pl.when(pl.program_id(2) == 0)
    def _():
        st_ref[...] = jnp.zeros_like(st_ref)

    for r in range(0, o_ref.shape[0], GLA_TILE):
        rows = slice(r, r + GLA_TILE)
        for hh in range(HGRN_HEADS_PER_STEP):
            lb = lb_ref[hh]
            zf = f_ref[hh, 0, rows, :].astype(F32)
            t = jnp.exp(-jnp.abs(zf))
            inv = 1.0 / (1.0 + t)
            pos = zf >= 0.0
            sig = jnp.where(pos, 1.0, t) * inv
            log_f = jnp.where(lb > 0.0, jnp.log(lb + (1.0 - lb) * sig),
                              jnp.minimum(zf, 0.0) - jnp.log(1.0 + t))
            k = (1.0 - lb) * (jnp.where(pos, t, 1.0) * inv)
            o = _gla_block(q_ref[hh, 0, rows, :].astype(F32), k, i_ref[hh, 0, rows, :], log_f, st_ref.at[hh])
            o = o * _sigmoid(g_ref[hh, 0, rows, :].astype(F32))
            o_ref[rows, hh * DV_B:(hh + 1) * DV_B] = (_rms(o) * og_ref[...]).astype(o_ref.dtype)


def _hgrn(u4, lb, out_gain, tc):
    _, bsz, seq, _ = u4.shape
    nt = seq // tc
    hp = HGRN_HEADS_PER_STEP
    blk = lambda base: pl.BlockSpec((hp, 1, tc, LANE), lambda b, h, i: (base // hp + h, b, i, 0))
    return pl.pallas_call(
        _hgrn_kernel,
        grid=(bsz, H_B // hp, nt),
        in_specs=[
            pl.BlockSpec((hp, 1, DK_B), lambda b, h, i: (h, 0, 0)),
            pl.BlockSpec((1, DV_B), lambda b, h, i: (0, 0)),
            blk(BLK_BQ), blk(BLK_BF), blk(BLK_BI), blk(BLK_BG),
        ],
        out_specs=pl.BlockSpec((tc, hp * DV_B), lambda b, h, i: (b * nt + i, h)),
        out_shape=jax.ShapeDtypeStruct((bsz * seq, MIX_WIDTH), BF16),
        scratch_shapes=[pltpu.VMEM((hp, DV_B, DK_B), F32)],
        compiler_params=_params("parallel", "parallel", "arbitrary"),
        name="hgrn2",
    )(lb.reshape(H_B, 1, DK_B), out_gain.reshape(1, DV_B), u4, u4, u4, u4)


def _glac_kernel(wup_ref, bup_ref, og_ref, q_ref, k_ref, v_ref, r_ref, gd_ref, o_ref, st_ref):
    @pl.when(pl.program_id(2) == 0)
    def _():
        st_ref[...] = jnp.zeros_like(st_ref)

    for r in range(0, o_ref.shape[0], GLA_TILE):
        rows = slice(r, r + GLA_TILE)
        for hh in range(GLA_HEADS_PER_STEP):
            z = _dot(gd_ref[0, 0, rows, :], wup_ref[:, hh * DK_C:(hh + 1) * DK_C]) + bup_ref[hh]
            log_a = _log_sigmoid(z) * (1.0 / GLA_GATE_NORM)
            q = q_ref[hh, 0, rows, :].astype(F32) * (DK_C ** -0.5)
            v = jnp.concatenate([v_ref[2 * hh, 0, rows, :], v_ref[2 * hh + 1, 0, rows, :]], axis=1)
            o = _gla_block(q, k_ref[hh, 0, rows, :].astype(F32), v, log_a, st_ref.at[hh])
            gate = jnp.concatenate([r_ref[2 * hh, 0, rows, :], r_ref[2 * hh + 1, 0, rows, :]], axis=1).astype(F32)
            o_ref[rows, hh * DV_C:(hh + 1) * DV_C] = (
                _rms(o) * og_ref[...] * (gate * _sigmoid(gate))).astype(o_ref.dtype)


def _glac(u4, w_up_p, b_up, out_gain, tc):
    _, bsz, seq, _ = u4.shape
    nt = seq // tc
    hp = GLA_HEADS_PER_STEP
    one = lambda base: pl.BlockSpec((hp, 1, tc, LANE), lambda b, h, i: (base // hp + h, b, i, 0))
    two = lambda base: pl.BlockSpec((2 * hp, 1, tc, LANE), lambda b, h, i: (base // (2 * hp) + h, b, i, 0))
    return pl.pallas_call(
        _glac_kernel,
        grid=(bsz, H_C // hp, nt),
        in_specs=[
            pl.BlockSpec((LANE, hp * DK_C), lambda b, h, i: (0, h)),
            pl.BlockSpec((hp, 1, DK_C), lambda b, h, i: (h, 0, 0)),
            pl.BlockSpec((1, DV_C), lambda b, h, i: (0, 0)),
            one(BLK_CQ), one(BLK_CK), two(BLK_CV), two(BLK_CR),
            pl.BlockSpec((1, 1, tc, LANE), lambda b, h, i: (BLK_CGD, b, i, 0)),
        ],
        out_specs=pl.BlockSpec((tc, hp * DV_C), lambda b, h, i: (b * nt + i, h)),
        out_shape=jax.ShapeDtypeStruct((bsz * seq, MIX_WIDTH), BF16),
        scratch_shapes=[pltpu.VMEM((hp, DV_C, DK_C), F32)],
        compiler_params=_params("parallel", "parallel", "arbitrary"),
        name="gla",
    )(w_up_p, b_up.reshape(H_C, 1, DK_C), out_gain.reshape(1, DV_C), u4, u4, u4, u4, u4)


def _merge_kernel(ya_ref, yb_ref, yc_ref, gl_ref, wb_ref, wo_ref, x_ref, gate_ref, o_ref):
    n = pl.program_id(1)
    m_ref = o_ref

    def contribution(y_ref, branch):
        z = _dot(y_ref[...], wb_ref[branch])
        cols = []
        for c in range(D_MODEL // LANE):
            cols.append(_sigmoid(gl_ref[c].astype(F32)) * z[:, c * LANE:(c + 1) * LANE])
        return jnp.concatenate(cols, axis=1)

    @pl.when(n == 0)
    def _():
        m_ref[...] = contribution(ya_ref, 0)

    @pl.when(n == 1)
    def _():
        m_ref[...] += contribution(yb_ref, 1)

    @pl.when(n == 2)
    def _():
        m = m_ref[...] + contribution(yc_ref, 2)
        o_ref[...] = x_ref[...] + gate_ref[0] * _dot(m.astype(BF16), wo_ref[...])


def _merge(ya, yb, yc, u3, w_branch, w_out, x2, gate, seq):
    t = x2.shape[0]
    tm = min(MERGE_TM, seq)
    per_b = seq // tm
    ysp = pl.BlockSpec((tm, MIX_WIDTH), lambda i, n: (i, 0))
    row = pl.BlockSpec((tm, D_MODEL), lambda i, n: (i, 0))
    return pl.pallas_call(
        _merge_kernel,
        grid=(t // tm, N_BRANCH),
        in_specs=[
            ysp, ysp, ysp,
            pl.BlockSpec((D_MODEL // LANE, tm, LANE), lambda i, n: (n, i, 0)),
            pl.BlockSpec((N_BRANCH, MIX_WIDTH, D_MODEL), lambda i, n: (0, 0, 0), pipeline_mode=pl.Buffered(1)),
            pl.BlockSpec((D_MODEL, D_MODEL), lambda i, n: (0, 0), pipeline_mode=pl.Buffered(1)),
            row,
            pl.BlockSpec((1, 1, D_MODEL), lambda i, n: (i // per_b, 0, 0)),
        ],
        out_specs=row,
        out_shape=jax.ShapeDtypeStruct((t, D_MODEL), F32),
        compiler_params=_params("parallel", "arbitrary"),
        name="merge_out",
    )(ya, yb, yc, u3, w_branch, w_out, x2, gate)


def kernel(x, c, w_ada, b_ada, norm_gains, ffn_w_gate, ffn_w_up, ffn_w_down, w_in, qk_gains, diff_lambda,
           diff_out_gain, rel_bias, hgrn_lb_logits, hgrn_out_gain, gla_w_gate_up, gla_b_gate, gla_out_gain,
           w_branch, w_out):
    bsz, seq, _ = x.shape
    t = bsz * seq
    tq = min(ATTN_TQ, seq)
    tc = min(GLA_TC, seq)

    lb_all = jnp.cumsum(jax.nn.softmax(hgrn_lb_logits.astype(F32), axis=0), axis=0)
    lb_all = lb_all - lb_all[0]
    mod = _ada(c, w_ada, b_ada).reshape(DEPTH, bsz, 3, 3, 1, D_MODEL)
    bias0, bias1 = _bias_tiles(rel_bias, tq)

    x2 = x.reshape(t, D_MODEL)
    for l in range(DEPTH):
        shift, scale, gate = mod[l, :, :, 0], mod[l, :, :, 1], mod[l, :, :, 2]
        gains = norm_gains[l].reshape(4, 1, D_MODEL)
        lam_init = 0.8 - 0.6 * math.exp(-0.3 * l)

        x2 = _ffn(x2, seq, shift[:, 0], scale[:, 0], gate[:, 0], gains[0], gains[3],
                  ffn_w_gate[l, 0].astype(BF16), ffn_w_up[l, 0].astype(BF16), ffn_w_down[l, 0].astype(BF16),
                  final_norm=False)

        w_l = w_in[l]
        n_real = N_MAIN + GLA_GATE_RANK
        w_in_p = jnp.concatenate(
            [w_l[:, n_real:].astype(BF16), w_l[:, :n_real].astype(BF16),
             jnp.zeros((D_MODEL, N_IN_PAD - w_l.shape[1]), BF16)], axis=1)
        u3 = _inproj(x2, seq, shift[:, 1], scale[:, 1], gains[1], w_in_p)
        u4 = u3.reshape(N_BLK, bsz, seq, LANE)

        ya = _attention(u4, diff_lambda[l], diff_out_gain[l], qk_gains[l], bias0, bias1, lam_init=lam_init)
        yb = _hgrn(u4, lb_all[l], hgrn_out_gain[l], tc)
        w_up_p = jnp.concatenate(
            [gla_w_gate_up[l], jnp.zeros((LANE - GLA_GATE_RANK, H_C * DK_C), F32)], axis=0).astype(BF16)
        yc = _glac(u4, w_up_p, gla_b_gate[l], gla_out_gain[l], tc)

        x2 = _merge(ya, yb, yc, u3, w_branch[l].astype(BF16), w_out[l].astype(BF16), x2, gate[:, 1], seq)

        x2 = _ffn(x2, seq, shift[:, 2], scale[:, 2], gate[:, 2], gains[2], gains[3],
                  ffn_w_gate[l, 1].astype(BF16), ffn_w_up[l, 1].astype(BF16), ffn_w_down[l, 1].astype(BF16),
                  final_norm=True)
    return x2.reshape(bsz, seq, D_MODEL)
```
